```python
import math
import jax, jax.numpy as jnp
from jax import lax
import numpy as np

D_MODEL = 2048
BATCH = 4
SEQ = 2048
DEPTH = 1
DEC_BATCH = 2
DEC_SEQ = 4096
PAST_LEN = 128

ATT_HEADS = 16
HEAD_DIM = 64
ATT_WIDTH = ATT_HEADS * HEAD_DIM
HYENA_WIDTH = D_MODEL - ATT_WIDTH
IN_WIDTH = 3 * ATT_WIDTH + 3 * HYENA_WIDTH

HYENA_EMB_DIM = 33
HYENA_FILTER_WIDTH = 64
HYENA_DECAY_TARGET = 1e-2
HYENA_FAST_DECAY = 0.3
HYENA_SLOW_DECAY = 1.5

DILATED_PATTERNS = ((128, 1), (512, 4), (2048, 16))
REL_BUCKETS = 32
REL_MAX_DISTANCE = 1024

PEER_KEYS = 128
PEER_EXPERTS = PEER_KEYS * PEER_KEYS
PEER_HEADS = 8
PEER_TOPK = 16
PEER_QUERY_DIM = 256
PEER_CHUNK = 128

RMS_EPS = 1e-6
NEG_INF = -1e30

kernel_name = "hybrid_hyena_dilated_attn_peer_encoder"


def rms_norm(x, gain):
    xf = x.astype(jnp.float32)
    y = xf * lax.rsqrt(jnp.mean(xf * xf, axis=-1, keepdims=True) + RMS_EPS)
    return (y * gain.astype(jnp.float32)).astype(x.dtype)


def t5_bucket(rel):
    half = REL_BUCKETS // 2
    exact = half // 2
    n = jnp.abs(rel)
    nf = jnp.maximum(n, 1).astype(jnp.float32)
    large = exact + (jnp.log(nf / exact) / math.log(REL_MAX_DISTANCE / exact) * (half - exact)).astype(jnp.int32)
    large = jnp.minimum(large, half - 1)
    return jnp.where(rel > 0, half, 0) + jnp.where(n < exact, n, large)


def dilated_window_attention(q, k, v, rel_bias, window, dilation):
    b, L, h, dh = q.shape
    w = window // (2 * dilation)
    m = L // dilation
    nb = -(-m // w)
    mp = nb * w

    def to_sub(t):
        return t.reshape(b, m, dilation, h, dh).transpose(0, 2, 3, 1, 4)

    qs, ks, vs = to_sub(q), to_sub(k), to_sub(v)
    qb = jnp.pad(qs, ((0, 0),) * 3 + ((0, mp - m), (0, 0))).reshape(b, dilation, h, nb, w, dh)
    pad_kv = ((0, 0),) * 3 + ((w, mp - m + w), (0, 0))
    kp, vp = jnp.pad(ks, pad_kv), jnp.pad(vs, pad_kv)

    def blocks(t):
        return jnp.concatenate(
            [t[..., j * w:j * w + mp, :].reshape(b, dilation, h, nb, w, dh) for j in range(3)], axis=-2)

    kb, vb = blocks(kp), blocks(vp)
    qq = jnp.arange(w)[:, None]
    kk = jnp.arange(3 * w)[None, :]
    rel = kk - w - qq
    band = jnp.abs(rel) <= w
    bias = rel_bias.astype(jnp.float32)[t5_bucket(rel * dilation)].transpose(2, 0, 1)
    key_pos = jnp.arange(nb)[:, None] * w + jnp.arange(3 * w)[None, :] - w
    key_ok = (key_pos >= 0) & (key_pos < m)
    mask = band[None] & key_ok[:, None, :]

    s = jnp.einsum('brhnqe,brhnke->brhnqk', qb, kb) + bias[:, None]
    s = jnp.where(mask, s, NEG_INF)
    mx = s.max(-1)
    p = jnp.exp(s - mx[..., None])
    den = p.sum(-1)
    o = jnp.einsum('brhnqk,brhnke->brhnqe', p, vb) / den[..., None]

    o = o.reshape(b, dilation, h, mp, dh)[..., :m, :].transpose(0, 3, 1, 2, 4).reshape(b, L, h, dh)
    mx = mx.reshape(b, dilation, h, mp)[..., :m].transpose(0, 3, 1, 2).reshape(b, L, h)
    den = den.reshape(b, dilation, h, mp)[..., :m].transpose(0, 3, 1, 2).reshape(b, L, h)
    return o, mx, den


def dilated_mixture_attention(q, k, v, rel_bias):
    res = [dilated_window_attention(q, k, v, rel_bias, wn, dl) for wn, dl in DILATED_PATTERNS]
    outs = jnp.stack([r[0] for r in res])
    mxs = jnp.stack([r[1] for r in res])
    dens = jnp.stack([r[2] for r in res])
    wts = dens * jnp.exp(mxs - mxs.max(0, keepdims=True))
    wts = wts / wts.sum(0, keepdims=True)
    return jnp.einsum('gblh,gblhe->blhe', wts, outs)


def hyena_filters(L, w1, b1, w2, b2, w3, b3, sin_freq, decay):
    f32 = jnp.float32
    t = jnp.linspace(0.0, 1.0, L, dtype=f32)[:, None]
    bands = (HYENA_EMB_DIM - 1) // 2
    freqs = jnp.linspace(1e-4, bands - 1, bands, dtype=f32)[None, :]
    wpos = 2.0 * math.pi * jnp.arange(L, dtype=f32)[:, None] / L
    z = jnp.concatenate([t, jnp.cos(freqs * wpos), -jnp.sin(freqs * wpos)], axis=-1)
    sf = sin_freq.astype(f32)
    hdn = jnp.sin(sf[0] * (z @ w1.astype(f32) + b1.astype(f32)))
    hdn = jnp.sin(sf[1] * (hdn @ w2.astype(f32) + b2.astype(f32)))
    hf = (hdn @ w3.astype(f32) + b3.astype(f32)).reshape(L, 2, HYENA_WIDTH)
    hf = hf * jnp.exp(-t[:, :, None] * jnp.abs(decay.astype(f32))[None])
    fwd, bwd = hf[:, 0], hf[:, 1]
    return jnp.concatenate([fwd, jnp.zeros_like(fwd[:1]), bwd[:0:-1]], axis=0)


def hyena_mixer(u, short_w, short_b, w1, b1, w2, b2, w3, b3, sin_freq, decay, bias_d):
    L = u.shape[1]
    up = jnp.pad(u, ((0, 0), (1, 1), (0, 0)))
    uc = up[:, :-2] * short_w[0] + up[:, 1:-1] * short_w[1] + up[:, 2:] * short_w[2] + short_b
    x0, x1, v = jnp.split(uc, 3, axis=-1)
    filt = hyena_filters(L, w1, b1, w2, b2, w3, b3, sin_freq, decay)
    z = (v * x1).astype(jnp.float32)
    zf = jnp.fft.rfft(z, n=2 * L, axis=1)
    kf = jnp.fft.rfft(filt, n=2 * L, axis=0)
    y = jnp.fft.irfft(zf * kf[None], n=2 * L, axis=1)[:, :L] + z * bias_d.astype(jnp.float32)
    return (x0.astype(jnp.float32) * y).astype(u.dtype)


def peer_ffn(h, w_query, sub_keys, u_tab, v_tab):
    b, L, d = h.shape
    T = b * L
    t = h.reshape(T, d)
    qry = (t @ w_query).astype(jnp.float32).reshape(T, PEER_HEADS, 2, PEER_QUERY_DIM // 2)
    s = jnp.einsum('thpe,hpne->thpn', qry, sub_keys.astype(jnp.float32))
    sv, si = lax.top_k(s, PEER_TOPK)
    cand = sv[..., 0, :, None] + sv[..., 1, None, :]
    cidx = si[..., 0, :, None] * PEER_KEYS + si[..., 1, None, :]
    bv, bi = lax.top_k(cand.reshape(T, PEER_HEADS, PEER_TOPK * PEER_TOPK), PEER_TOPK)
    eidx = jnp.take_along_axis(cidx.reshape(T, PEER_HEADS, PEER_TOPK * PEER_TOPK), bi, axis=-1)
    gate = jax.nn.softmax(bv, axis=-1)
    nsel = PEER_HEADS * PEER_TOPK
    n_chunks = T // PEER_CHUNK

    def chunk(args):
        tc, ic, gc = args
        a = jnp.einsum('cd,cjd->cj', tc, u_tab[ic]).astype(jnp.float32)
        wgt = (gc * jax.nn.gelu(a)).astype(v_tab.dtype)
        return jnp.einsum('cj,cjd->cd', wgt, v_tab[ic])

    out = lax.map(chunk, (t.reshape(n_chunks, PEER_CHUNK, d),
                          eidx.reshape(n_chunks, PEER_CHUNK, nsel),
                          gate.reshape(n_chunks, PEER_CHUNK, nsel)))
    return out.reshape(b, L, d).astype(h.dtype)


def encoder_layer(x, c, rel_bias, w_ada, b_ada, norm1_g, w_in, q_norm_g, k_norm_g,
                  hyena_short_w, hyena_short_b, hyena_ffn_w1, hyena_ffn_b1, hyena_ffn_w2, hyena_ffn_b2,
                  hyena_ffn_w3, hyena_ffn_b3, hyena_sin_freq, hyena_decay, hyena_bias_d,
                  att_out_g, hyena_out_g, w_out, norm2_g, peer_w_query, peer_sub_keys, peer_u, peer_v):
    f32 = jnp.float32
    b, L, d = x.shape
    ada = jax.nn.silu(c.astype(f32)) @ w_ada.astype(f32) + b_ada.astype(f32)
    sh1, sc1, g1, sh2, sc2, g2 = jnp.split(ada, 6, axis=-1)

    hmod = (rms_norm(x, norm1_g).astype(f32) * (1.0 + sc1[:, None]) + sh1[:, None]).astype(x.dtype)
    proj = hmod @ w_in
    q = proj[..., :ATT_WIDTH].reshape(b, L, ATT_HEADS, HEAD_DIM)
    k = proj[..., ATT_WIDTH:2 * ATT_WIDTH].reshape(b, L, ATT_HEADS, HEAD_DIM)
    v = proj[..., 2 * ATT_WIDTH:3 * ATT_WIDTH].reshape(b, L, ATT_HEADS, HEAD_DIM)
    hy = proj[..., 3 * ATT_WIDTH:]
    q = rms_norm(q, q_norm_g).astype(f32) * (HEAD_DIM ** -0.5)
    k = rms_norm(k, k_norm_g).astype(f32)
    att = dilated_mixture_attention(q, k, v.astype(f32), rel_bias).reshape(b, L, ATT_WIDTH).astype(x.dtype)
    hyo = hyena_mixer(hy, hyena_short_w, hyena_short_b, hyena_ffn_w1, hyena_ffn_b1, hyena_ffn_w2,
                      hyena_ffn_b2, hyena_ffn_w3, hyena_ffn_b3, hyena_sin_freq, hyena_decay, hyena_bias_d)
    mix = jnp.concatenate([rms_norm(att, att_out_g), rms_norm(hyo, hyena_out_g)], axis=-1)
    x = x + (g1[:, None] * (mix @ w_out).astype(f32)).astype(x.dtype)

    h2 = (rms_norm(x, norm2_g).astype(f32) * (1.0 + sc2[:, None]) + sh2[:, None]).astype(x.dtype)
    y = peer_ffn(h2, peer_w_query, peer_sub_keys, peer_u, peer_v)
    return x + (g2[:, None] * y.astype(f32)).astype(x.dtype)


def setup_inputs(seed: int = 0) -> dict:
    key = jax.random.key(seed)
    ks = jax.random.split(key, 32)
    f32 = jnp.float32
    D, HW, FW = D_MODEL, HYENA_WIDTH, HYENA_FILTER_WIDTH

    def nrm(k, shape, scale):
        return jax.random.normal(k, shape, f32) * scale

    def gain(k, shape):
        return 1.0 + nrm(k, shape, 0.05)

    lo = abs(math.log(HYENA_DECAY_TARGET)) / HYENA_SLOW_DECAY
    hi = abs(math.log(HYENA_DECAY_TARGET)) / HYENA_FAST_DECAY
    decay_base = jnp.linspace(lo, hi, HW, dtype=f32)
    return {
        "x_prompt": nrm(ks[0], (BATCH, SEQ, D), 1.0),
        "x_sample": nrm(ks[1], (DEC_BATCH, DEC_SEQ, D), 1.0),
        "c_prompt": nrm(ks[2], (BATCH, D), 1.0),
        "c_sample": nrm(ks[3], (DEC_BATCH, D), 1.0),
        "rel_bias": nrm(ks[4], (REL_BUCKETS, ATT_HEADS), 0.2),
        "w_ada": nrm(ks[5], (DEPTH, D, 6 * D), 0.5 * D ** -0.5),
        "b_ada": nrm(ks[6], (DEPTH, 6 * D), 0.01),
        "norm1_g": gain(ks[7], (DEPTH, D)),
        "w_in": nrm(ks[8], (DEPTH, D, IN_WIDTH), D ** -0.5),
        "q_norm_g": gain(ks[9], (DEPTH, HEAD_DIM)),
        "k_norm_g": gain(ks[10], (DEPTH, HEAD_DIM)),
        "hyena_short_w": nrm(ks[11], (DEPTH, 3, 3 * HW), 3 ** -0.5),
        "hyena_short_b": nrm(ks[12], (DEPTH, 3 * HW), 0.02),
        "hyena_ffn_w1": nrm(ks[13], (DEPTH, HYENA_EMB_DIM, FW), HYENA_EMB_DIM ** -0.5),
        "hyena_ffn_b1": nrm(ks[14], (DEPTH, FW), 0.1),
        "hyena_ffn_w2": nrm(ks[15], (DEPTH, FW, FW), FW ** -0.5),
        "hyena_ffn_b2": nrm(ks[16], (DEPTH, FW), 0.1),
        "hyena_ffn_w3": nrm(ks[17], (DEPTH, FW, 2 * HW), FW ** -0.5),
        "hyena_ffn_b3": nrm(ks[18], (DEPTH, 2 * HW), 0.02),
        "hyena_sin_freq": gain(ks[19], (DEPTH, 2, FW)),
        "hyena_decay": decay_base[None, None] * (1.0 + nrm(ks[20], (DEPTH, 2, HW), 0.05)),
        "hyena_bias_d": nrm(ks[21], (DEPTH, HW), 0.1),
        "att_out_g": gain(ks[22], (DEPTH, ATT_WIDTH)),
        "hyena_out_g": gain(ks[23], (DEPTH, HW)),
        "w_out": nrm(ks[24], (DEPTH, D, D), D ** -0.5),
        "norm2_g": gain(ks[25], (DEPTH, D)),
        "peer_w_query": nrm(ks[26], (DEPTH, D, PEER_HEADS * PEER_QUERY_DIM), D ** -0.5),
        "peer_sub_keys": nrm(ks[27], (DEPTH, PEER_HEADS, 2, PEER_KEYS, PEER_QUERY_DIM // 2), (PEER_QUERY_DIM // 2) ** -0.5),
        "peer_u": nrm(ks[28], (DEPTH, PEER_EXPERTS, D), D ** -0.5),
        "peer_v": nrm(ks[29], (DEPTH, PEER_EXPERTS, D), 0.5),
    }


def reference(x_prompt, x_sample, c_prompt, c_sample, rel_bias, w_ada, b_ada, norm1_g, w_in,
              q_norm_g, k_norm_g, hyena_short_w, hyena_short_b, hyena_ffn_w1, hyena_ffn_b1,
              hyena_ffn_w2, hyena_ffn_b2, hyena_ffn_w3, hyena_ffn_b3, hyena_sin_freq, hyena_decay,
              hyena_bias_d, att_out_g, hyena_out_g, w_out, norm2_g, peer_w_query, peer_sub_keys,
              peer_u, peer_v):
    y_prompt, y_sample = x_prompt, x_sample
    for layer in range(DEPTH):
        lp = (w_ada[layer], b_ada[layer], norm1_g[layer], w_in[layer], q_norm_g[layer], k_norm_g[layer],
              hyena_short_w[layer], hyena_short_b[layer], hyena_ffn_w1[layer], hyena_ffn_b1[layer],
              hyena_ffn_w2[layer], hyena_ffn_b2[layer], hyena_ffn_w3[layer], hyena_ffn_b3[layer],
              hyena_sin_freq[layer], hyena_decay[layer], hyena_bias_d[layer], att_out_g[layer],
              hyena_out_g[layer], w_out[layer], norm2_g[layer], peer_w_query[layer],
              peer_sub_keys[layer], peer_u[layer], peer_v[layer])
        y_prompt = encoder_layer(y_prompt, c_prompt, rel_bias, *lp)
        y_sample = encoder_layer(y_sample, c_sample, rel_bias, *lp)
    return (y_prompt, y_sample)
```

```python
import functools
import math

import jax
import jax.numpy as jnp
from jax import lax
from jax.experimental import pallas as pl
from jax.experimental.pallas import tpu as pltpu

F32 = jnp.float32
BF16 = jnp.bfloat16
HIGHEST = lax.Precision.HIGHEST

HEAD_DIM = 64
RMS_EPS = 1e-6
MASKED = -1e30
DILATIONS = (1, 4, 16)
HALF_WINDOW = 64
MAX_DILATION = 16
REL_BUCKETS = 32
REL_MAX_DISTANCE = 1024
HYENA_EMB_DIM = 33
PEER_KEYS = 128
PEER_HEADS = 8
PEER_TOPK = 16

LANES = 128
V7X_VMEM_LIMIT_BYTES = 58 * 1024 * 1024


def _cparams(*sem):
    return pltpu.CompilerParams(dimension_semantics=sem, vmem_limit_bytes=V7X_VMEM_LIMIT_BYTES)


def _dot(a, b, **kw):
    return jnp.dot(a, b, preferred_element_type=F32, **kw)


def _rms(x):
    return x * lax.rsqrt(jnp.mean(x * x, axis=-1, keepdims=True) + RMS_EPS)


def _modulated_norm(x, gain, scale, shift):
    return (_rms(x) * gain) * (1.0 + scale) + shift


def _ada_kernel(c_ref, w_ref, b_ref, o_ref):
    c = c_ref[...]
    a = c / (1.0 + jnp.exp(-c))
    o_ref[...] = _dot(a, w_ref[...], precision=HIGHEST) + b_ref[...]


def _ada(c_all, w_ada, b_ada):
    nb, d = c_all.shape
    n = w_ada.shape[1]
    tn = 768
    return pl.pallas_call(
        _ada_kernel,
        grid=(n // tn,),
        in_specs=[pl.BlockSpec((nb, d), lambda j: (0, 0)),
                  pl.BlockSpec((d, tn), lambda j: (0, j)),
                  pl.BlockSpec((1, tn), lambda j: (0, j))],
        out_specs=pl.BlockSpec((nb, tn), lambda j: (0, j)),
        out_shape=jax.ShapeDtypeStruct((nb, n), F32),
        compiler_params=_cparams("arbitrary"),
        name="ada",
    )(c_all, w_ada, b_ada.reshape(1, n))


def _head_rms(y, gain, ones_bd):
    outs = []
    for c in range(y.shape[1] // LANES):
        yc = y[:, c * LANES:(c + 1) * LANES]
        sq = yc * yc
        hi = sq.astype(BF16)
        lo = (sq - hi.astype(F32)).astype(BF16)
        ss = _dot(hi, ones_bd) + _dot(lo, ones_bd)
        outs.append(yc * lax.rsqrt(ss * (1.0 / HEAD_DIM) + RMS_EPS))
    return jnp.concatenate(outs, axis=1) * gain


def _qkv_kernel(x_ref, ada_ref, g_ref, w_ref, gq_ref, gk_ref, bd_ref, *outs):
    h = _modulated_norm(x_ref[0], g_ref[...], ada_ref[0, 1:2, :], ada_ref[0, 0:1, :]).astype(BF16)
    y = _dot(h, w_ref[...])
    w = y.shape[1] // 3
    bd = bd_ref[...]
    q = _head_rms(y[:, :w], gq_ref[...], bd) * (HEAD_DIM ** -0.5)
    k = _head_rms(y[:, w:2 * w], gk_ref[...], bd)
    v = y[:, 2 * w:]
    for j, t in enumerate((q, k, v)):
        tb = t.astype(BF16)
        outs[3 * j][0] = tb
        outs[3 * j + 1][0, 0] = tb
        outs[3 * j + 2][0, 0] = tb


def _qkv_proj(x, ada_g, norm_g, w_qkv, gq, gk, ones_bd):
    b, L, d = x.shape
    r = MAX_DILATION
    m = L // r
    w = w_qkv.shape[1] // 3
    xv = x.reshape(b, m, r * d)
    shapes, specs = [], []
    for _ in range(3):
        shapes += [jax.ShapeDtypeStruct((b, m, r * w), BF16),
                   jax.ShapeDtypeStruct((b, 4, m, 4 * w), BF16),
                   jax.ShapeDtypeStruct((b, r, m, w), BF16)]
        specs += [pl.BlockSpec((1, m, w), lambda bi, ri: (bi, 0, ri)),
                  pl.BlockSpec((1, 1, m, w), lambda bi, ri: (bi, ri % 4, 0, ri // 4)),
                  pl.BlockSpec((1, 1, m, w), lambda bi, ri: (bi, ri, 0, 0))]
    outs = pl.pallas_call(
        _qkv_kernel,
        grid=(b, r),
        in_specs=[pl.BlockSpec((1, m, d), lambda bi, ri: (bi, 0, ri)),
                  pl.BlockSpec((1, 6, d), lambda bi, ri: (bi, 0, 0)),
                  pl.BlockSpec((1, d), lambda bi, ri: (0, 0)),
                  pl.BlockSpec((d, 3 * w), lambda bi, ri: (0, 0)),
                  pl.BlockSpec((1, w), lambda bi, ri: (0, 0)),
                  pl.BlockSpec((1, w), lambda bi, ri: (0, 0)),
                  pl.BlockSpec((LANES, LANES), lambda bi, ri: (0, 0))],
        out_specs=specs,
        out_shape=shapes,
        compiler_params=_cparams("arbitrary", "arbitrary"),
        name="qkv_proj",
    )(xv, ada_g, norm_g, w_qkv, gq, gk, ones_bd)
    res = []
    for j in range(3):
        nat, p4, p16 = outs[3 * j:3 * j + 3]
        res.append((nat.reshape(b, L, w), p4.reshape(b * 4, L // 4, w), p16.reshape(b * r, m, w)))
    return res


def _hyproj_kernel(x_ref, ada_ref, g_ref, w_ref, o_ref):
    h = _modulated_norm(x_ref[0], g_ref[...], ada_ref[0, 1:2, :], ada_ref[0, 0:1, :]).astype(BF16)
    o_ref[0] = _dot(h, w_ref[...]).astype(BF16)


def _hy_proj(x, ada_g, norm_g, w_hy):
    b, L, d = x.shape
    n = w_hy.shape[1]
    tm = 512
    return pl.pallas_call(
        _hyproj_kernel,
        grid=(b, L // tm),
        in_specs=[pl.BlockSpec((1, tm, d), lambda bi, i: (bi, i, 0)),
                  pl.BlockSpec((1, 6, d), lambda bi, i: (bi, 0, 0)),
                  pl.BlockSpec((1, d), lambda bi, i: (0, 0)),
                  pl.BlockSpec((d, n), lambda bi, i: (0, 0))],
        out_specs=pl.BlockSpec((1, tm, n), lambda bi, i: (bi, i, 0)),
        out_shape=jax.ShapeDtypeStruct((b, L, n), BF16),
        compiler_params=_cparams("arbitrary", "arbitrary"),
        name="hy_proj",
    )(x, ada_g, norm_g, w_hy)


def _att_kernel(q_ref, kp_ref, kc_ref, kn_ref, vp_ref, vc_ref, vn_ref, bias_ref, o_ref, l_ref):
    width = q_ref.shape[2]
    first = lax.broadcasted_iota(jnp.int32, (1, LANES), 1) < HEAD_DIM
    for hp in range(width // LANES):
        sl = slice(hp * LANES, (hp + 1) * LANES)
        q = q_ref[0, :, sl]
        k = jnp.concatenate([kp_ref[0, :, sl], kc_ref[0, :, sl], kn_ref[0, :, sl]], axis=0)
        v = jnp.concatenate([vp_ref[0, :, sl], vc_ref[0, :, sl], vn_ref[0, :, sl]], axis=0)
        o_pair, l_pair = [], []
        for hh in range(2):
            sel = first if hh == 0 else jnp.logical_not(first)
            qh = jnp.where(sel, q, jnp.zeros_like(q))
            s = lax.dot_general(qh, k, (((1,), (1,)), ((), ())), preferred_element_type=F32)
            s = s + bias_ref[0, 2 * hp + hh]
            mx = jnp.max(s, axis=-1, keepdims=True)
            p = jnp.exp(s - mx)
            den = jnp.sum(p, axis=-1, keepdims=True)
            o_pair.append(_dot(p.astype(BF16), v) / den)
            l_pair.append(mx + jnp.log(den))
        o_ref[0, :, sl] = jnp.where(first, o_pair[0], o_pair[1])
        l_ref[0, :, sl] = jnp.where(first, l_pair[0], l_pair[1])


def _t5_bucket_of(rel):
    half = REL_BUCKETS // 2
    exact = half // 2
    n = jnp.abs(rel)
    nf = jnp.maximum(n, 1).astype(F32)
    large = exact + (jnp.log(nf / exact) / math.log(REL_MAX_DISTANCE / exact) * (half - exact)).astype(jnp.int32)
    large = jnp.minimum(large, half - 1)
    return jnp.where(rel > 0, half, 0) + jnp.where(n < exact, n, large)


def _band_bias(rel_bias, dilation):
    tq, tk, hw = 2 * HALF_WINDOW, 4 * HALF_WINDOW, HALF_WINDOW
    i = jnp.arange(tq)[:, None]
    j = jnp.arange(tk)[None, :]
    d = j - hw - i
    bias = rel_bias.astype(F32)[_t5_bucket_of(d * dilation)].transpose(2, 0, 1)
    band = jnp.abs(d) <= hw
    out = []
    for var in range(4):
        ok = band
        if var & 1:
            ok = ok & (j >= hw)
        if var & 2:
            ok = ok & (j < tk - hw)
        out.append(jnp.where(ok[None], bias, MASKED))
    return jnp.stack(out)


def _band_attention(q, k, v, bias, batch, dilation):
    bp, m, w = q.shape
    tq, hw = 2 * HALF_WINDOW, HALF_WINDOW
    nq = m // tq
    nh = w // HEAD_DIM

    def var_idx(qi):
        return (qi == 0).astype(jnp.int32) + 2 * (qi == nq - 1).astype(jnp.int32)

    prev = lambda b, qi: (b, jnp.maximum(2 * qi - 1, 0), 0)
    cur = lambda b, qi: (b, qi, 0)
    nxt = lambda b, qi: (b, jnp.minimum(2 * qi + 2, 2 * nq - 1), 0)
    out_map = lambda b, qi: (b // dilation, qi, b % dilation)
    o, l = pl.pallas_call(
        _att_kernel,
        grid=(bp, nq),
        in_specs=[pl.BlockSpec((1, tq, w), cur),
                  pl.BlockSpec((1, hw, w), prev), pl.BlockSpec((1, tq, w), cur), pl.BlockSpec((1, hw, w), nxt),
                  pl.BlockSpec((1, hw, w), prev), pl.BlockSpec((1, tq, w), cur), pl.BlockSpec((1, hw, w), nxt),
                  pl.BlockSpec((1, nh, tq, 2 * tq), lambda b, qi: (var_idx(qi), 0, 0, 0))],
        out_specs=[pl.BlockSpec((1, tq, w), out_map), pl.BlockSpec((1, tq, w), out_map)],
        out_shape=[jax.ShapeDtypeStruct((batch, m, dilation * w), F32)] * 2,
        compiler_params=_cparams("arbitrary", "arbitrary"),
        name=f"band_att_d{dilation}",
    )(q, k, k, k, v, v, v, bias)
    L = m * dilation
    return o.reshape(batch, L, w), l.reshape(batch, L, w)


def _hy_pre_kernel(x0_ref, x1_ref, v_ref, w_ref, b_ref, z_ref, x0c_ref):
    L, tn = x0_ref.shape[1], x0_ref.shape[2]
    row = lax.broadcasted_iota(jnp.int32, (L, tn), 0)

    def conv(ref, s):
        u = ref[0].astype(F32)
        um = jnp.where(row == 0, 0.0, pltpu.roll(u, 1, 0))
        up = jnp.where(row == L - 1, 0.0, pltpu.roll(u, L - 1, 0))
        return um * w_ref[0, s:s + 1, :] + u * w_ref[1, s:s + 1, :] + up * w_ref[2, s:s + 1, :] + b_ref[s:s + 1, :]

    x0c_ref[0] = conv(x0_ref, 0).astype(BF16)
    z_ref[0] = (conv(v_ref, 2) * conv(x1_ref, 1)).astype(BF16)


def _hy_pre(hy, short_w, short_b):
    b, L, n3 = hy.shape
    c = n3 // 3
    tn = LANES
    nt = c // tn
    w3 = short_w.reshape(3, 3, c)
    b3 = short_b.reshape(3, c)
    stream = lambda s: pl.BlockSpec((1, L, tn), lambda bi, n: (bi, 0, s * nt + n))
    return pl.pallas_call(
        _hy_pre_kernel,
        grid=(b, nt),
        in_specs=[stream(0), stream(1), stream(2),
                  pl.BlockSpec((3, 3, tn), lambda bi, n: (0, 0, n)),
                  pl.BlockSpec((3, tn), lambda bi, n: (0, n))],
        out_specs=[pl.BlockSpec((1, L, tn), lambda bi, n: (bi, 0, n))] * 2,
        out_shape=[jax.ShapeDtypeStruct((b, L, c), BF16)] * 2,
        compiler_params=_cparams("arbitrary", "arbitrary"),
        name="hy_pre",
    )(hy, hy, hy, w3, b3)


def _filt_kernel(z_ref, w1_ref, b1_ref, w2_ref, b2_ref, w3_ref, b3_ref, sf_ref, dec_ref, o_ref):
    z = z_ref[...]
    tl = z.shape[0]
    c = dec_ref.shape[1]
    h = jnp.sin(sf_ref[0:1, :] * (_dot(z, w1_ref[...], precision=HIGHEST) + b1_ref[...]))
    h = jnp.sin(sf_ref[1:2, :] * (_dot(h, w2_ref[...], precision=HIGHEST) + b2_ref[...]))
    hf = _dot(h, w3_ref[...], precision=HIGHEST) + b3_ref[...]
    t = z[:, 0:1]
    dec = jnp.abs(dec_ref[...])
    fwd = hf[:, :c] * jnp.exp(-t * dec[0:1, :])
    bwd = hf[:, c:] * jnp.exp(-t * dec[1:2, :])
    row = pl.program_id(0) * tl + lax.broadcasted_iota(jnp.int32, (tl, c), 0)
    o_ref[0] = fwd.astype(BF16)
    o_ref[1] = jnp.where(row == 0, 0.0, bwd).astype(BF16)


def _position_embedding(L):
    t = jnp.linspace(0.0, 1.0, L, dtype=F32)[:, None]
    bands = (HYENA_EMB_DIM - 1) // 2
    freqs = jnp.linspace(1e-4, bands - 1, bands, dtype=F32)[None, :]
    wpos = 2.0 * math.pi * jnp.arange(L, dtype=F32)[:, None] / L
    z = jnp.concatenate([t, jnp.cos(freqs * wpos), -jnp.sin(freqs * wpos)], axis=-1)
    return jnp.pad(z, ((0, 0), (0, LANES - HYENA_EMB_DIM)))


def _hyena_filter_taps(L, w1, b1, w2, b2, w3, b3, sin_freq, decay):
    fw = w1.shape[1]
    c = decay.shape[1]
    tl = 512
    zemb = _position_embedding(L)
    w1p = jnp.pad(w1, ((0, LANES - HYENA_EMB_DIM), (0, 0)))
    full = lambda shape: pl.BlockSpec(shape, lambda i: (0,) * len(shape))
    return pl.pallas_call(
        _filt_kernel,
        grid=(L // tl,),
        in_specs=[pl.BlockSpec((tl, LANES), lambda i: (i, 0)),
                  full((LANES, fw)), full((1, fw)), full((fw, fw)), full((1, fw)),
                  full((fw, 2 * c)), full((1, 2 * c)), full((2, fw)), full((2, c))],
        out_specs=pl.BlockSpec((2, tl, c), lambda i: (0, i, 0)),
        out_shape=jax.ShapeDtypeStruct((2, L, c), BF16),
        compiler_params=_cparams("arbitrary"),
        name="hy_filter_taps",
    )(zemb, w1p, b1.reshape(1, fw), w2, b2.reshape(1, fw), w3, b3.reshape(1, 2 * c), sin_freq, decay)


def _shifted_dft_matrix(L):
    odd = 2 * jnp.arange(L, dtype=jnp.int32) + 1
    phase = (odd[:, None] * odd[None, :]) % (8 * L)
    ang = phase.astype(F32) * (2.0 * math.pi / (8 * L))
    return jnp.stack([jnp.cos(ang), jnp.sin(ang)]).astype(BF16)


def _filt_dft_kernel(m_ref, r_ref, ct_ref, st_ref, h_ref):
    a = _dot(m_ref[0], r_ref[0])
    b = _dot(m_ref[1], r_ref[0])
    c = _dot(m_ref[0], r_ref[1])
    d = _dot(m_ref[1], r_ref[1])
    ct, st = ct_ref[...], st_ref[...]
    h_ref[0] = ct * (a + c) + st * (b + d)
    h_ref[1] = st * (a - c) + ct * (d - b)


def _filter_spectrum(dft, taps):
    _, L, c = taps.shape
    tk, tn = 256, min(256, c)
    theta = (2 * jnp.arange(L, dtype=F32) + 1) * (math.pi / (4 * L))
    ct = jnp.cos(theta)[:, None]
    st = jnp.sin(theta)[:, None]
    return pl.pallas_call(
        _filt_dft_kernel,
        grid=(c // tn, L // tk),
        in_specs=[pl.BlockSpec((2, tk, L), lambda n, k: (0, k, 0)),
                  pl.BlockSpec((2, L, tn), lambda n, k: (0, 0, n)),
                  pl.BlockSpec((tk, 1), lambda n, k: (k, 0)),
                  pl.BlockSpec((tk, 1), lambda n, k: (k, 0))],
        out_specs=pl.BlockSpec((2, tk, tn), lambda n, k: (0, k, n)),
        out_shape=jax.ShapeDtypeStruct((2, L, c), F32),
        compiler_params=_cparams("arbitrary", "arbitrary"),
        name="hy_filter_dft",
    )(dft, taps, ct, st)


def _hy_fwd_kernel(m_ref, z_ref, h_ref, y_ref):
    zc = _dot(m_ref[0], z_ref[0])
    zs = _dot(m_ref[1], z_ref[0])
    hr, hi = h_ref[0], h_ref[1]
    y_ref[0, 0] = (zc * hr + zs * hi).astype(BF16)
    y_ref[0, 1] = (zc * hi - zs * hr).astype(BF16)


def _hy_forward(dft, z, spec):
    b, L, c = z.shape
    tk, tn = 256, min(512, c)
    return pl.pallas_call(
        _hy_fwd_kernel,
        grid=(b, c // tn, L // tk),
        in_specs=[pl.BlockSpec((2, tk, L), lambda bi, n, k: (0, k, 0)),
                  pl.BlockSpec((1, L, tn), lambda bi, n, k: (bi, 0, n)),
                  pl.BlockSpec((2, tk, tn), lambda bi, n, k: (0, k, n))],
        out_specs=pl.BlockSpec((1, 2, tk, tn), lambda bi, n, k: (bi, 0, k, n)),
        out_shape=jax.ShapeDtypeStruct((b, 2, L, c), BF16),
        compiler_params=_cparams("arbitrary", "arbitrary", "arbitrary"),
        name="hy_dft_fwd",
    )(dft, z, spec)


def _hy_inv_kernel(m_ref, y_ref, z_ref, x0_ref, bd_ref, o_ref, *, scale):
    conv = _dot(m_ref[0], y_ref[0, 0]) - _dot(m_ref[1], y_ref[0, 1])
    y = conv * scale + z_ref[0].astype(F32) * bd_ref[...]
    o_ref[0] = x0_ref[0].astype(F32) * y


def _hy_inverse(dft, yspec, z, x0c, bias_d):
    b, L, c = z.shape
    tt, tn = 256, min(512, c)
    return pl.pallas_call(
        functools.partial(_hy_inv_kernel, scale=1.0 / L),
        grid=(b, c // tn, L // tt),
        in_specs=[pl.BlockSpec((2, tt, L), lambda bi, n, t: (0, t, 0)),
                  pl.BlockSpec((1, 2, L, tn), lambda bi, n, t: (bi, 0, 0, n)),
                  pl.BlockSpec((1, tt, tn), lambda bi, n, t: (bi, t, n)),
                  pl.BlockSpec((1, tt, tn), lambda bi, n, t: (bi, t, n)),
                  pl.BlockSpec((1, tn), lambda bi, n, t: (0, n))],
        out_specs=pl.BlockSpec((1, tt, tn), lambda bi, n, t: (bi, t, n)),
        out_shape=jax.ShapeDtypeStruct((b, L, c), F32),
        compiler_params=_cparams("arbitrary", "arbitrary", "arbitrary"),
        name="hy_dft_inv",
    )(dft, yspec, z, x0c, bias_d.reshape(1, c))


def _out_kernel(o1_ref, o2_ref, o3_ref, l1_ref, l2_ref, l3_ref, hy_ref, x_ref, ada_ref,
                ga_ref, gh_ref, w_ref, g2_ref, x1_ref, h2_ref):
    la, lb, lc = l1_ref[0], l2_ref[0], l3_ref[0]
    lm = jnp.maximum(jnp.maximum(la, lb), lc)
    wa, wb, wc = jnp.exp(la - lm), jnp.exp(lb - lm), jnp.exp(lc - lm)
    att = (wa * o1_ref[0] + wb * o2_ref[0] + wc * o3_ref[0]) / (wa + wb + wc)
    wa_ = att.shape[1]
    an = (_rms(att) * ga_ref[...]).astype(BF16)
    hn = (_rms(hy_ref[0]) * gh_ref[...]).astype(BF16)
    res = _dot(an, w_ref[:wa_, :]) + _dot(hn, w_ref[wa_:, :])
    x1 = x_ref[0] + ada_ref[0, 2:3, :] * res
    x1_ref[0] = x1
    h2_ref[0] = _modulated_norm(x1, g2_ref[...], ada_ref[0, 4:5, :], ada_ref[0, 3:4, :]).astype(BF16)


def _out_proj(att_parts, hyo, x, ada_g, att_g, hy_g, w_out, norm2_g):
    b, L, d = x.shape
    wa = hyo.shape[2]
    tm = 256
    tok = lambda width: pl.BlockSpec((1, tm, width), lambda bi, i: (bi, i, 0))
    full = lambda shape: pl.BlockSpec(shape, lambda bi, i: (0,) * len(shape))
    (o1, l1), (o2, l2), (o3, l3) = att_parts
    return pl.pallas_call(
        _out_kernel,
        grid=(b, L // tm),
        in_specs=[tok(wa)] * 7 + [tok(d), pl.BlockSpec((1, 6, d), lambda bi, i: (bi, 0, 0)),
                                  full((1, wa)), full((1, wa)), full((d, d)), full((1, d))],
        out_specs=[tok(d), tok(d)],
        out_shape=[jax.ShapeDtypeStruct((b, L, d), F32), jax.ShapeDtypeStruct((b, L, d), BF16)],
        compiler_params=_cparams("arbitrary", "arbitrary"),
        name="out_proj",
    )(o1, o2, o3, l1, l2, l3, hyo, x, ada_g, att_g, hy_g, w_out, norm2_g)


def _peer_q_kernel(h_ref, w_ref, sk_ref, s_ref):
    q = _dot(h_ref[...], w_ref[...]).astype(BF16)
    for c in range(sk_ref.shape[0]):
        qc = q[:, c * PEER_KEYS:(c + 1) * PEER_KEYS]
        s_ref[c] = lax.dot_general(sk_ref[c], qc, (((1,), (1,)), ((), ())), preferred_element_type=F32)


def _peer_scores(h2, w_query, sub_keys):
    t, d = h2.shape
    nc, nk, qd = sub_keys.shape
    tm = 512
    return pl.pallas_call(
        _peer_q_kernel,
        grid=(t // tm,),
        in_specs=[pl.BlockSpec((tm, d), lambda i: (i, 0)),
                  pl.BlockSpec((d, nc * qd), lambda i: (0, 0)),
                  pl.BlockSpec((nc, nk, qd), lambda i: (0, 0, 0))],
        out_specs=pl.BlockSpec((nc, nk, tm), lambda i: (0, 0, i)),
        out_shape=jax.ShapeDtypeStruct((nc, nk, t), F32),
        compiler_params=_cparams("arbitrary"),
        name="peer_scores",
    )(h2, w_query, sub_keys)


def _gelu_tanh(x):
    return x * (0.5 * (1.0 + jnp.tanh(math.sqrt(2.0 / math.pi) * (x + 0.044715 * (x * x * x)))))


def _peer_main_kernel(ht_ref, u_ref, vt_ref, s1_ref, s2_ref, e1_ref, e2_ref, tau_ref, o_ref):
    e = pl.program_id(1)
    te = u_ref.shape[0]
    ni = te // PEER_KEYS

    @pl.when(e == 0)
    def _():
        o_ref[...] = jnp.zeros_like(o_ref)

    at = _dot(u_ref[...], ht_ref[...])
    parts = []
    for ii in range(ni):
        idx = e * ni + ii
        a = at[ii * PEER_KEYS:(ii + 1) * PEER_KEYS]
        w = jnp.zeros_like(a)
        for h in range(PEER_HEADS):
            pair = s1_ref[h, pl.ds(idx, 1), :] + s2_ref[h]
            gate = e1_ref[h, pl.ds(idx, 1), :] * e2_ref[h]
            w = w + jnp.where(pair >= tau_ref[h:h + 1, :], gate, 0.0)
        parts.append((_gelu_tanh(a) * w).astype(BF16))
    pt = jnp.concatenate(parts, axis=0)
    o_ref[...] += _dot(vt_ref[...], pt)


def _peer_experts(h2t, u_tab, vt_tab, s1, s2, e1, e2, tau):
    d, t = h2t.shape
    ne = u_tab.shape[0]
    tm, te = 512, 512
    gate = pl.BlockSpec((PEER_HEADS, PEER_KEYS, tm), lambda i, e: (0, 0, i))
    return pl.pallas_call(
        _peer_main_kernel,
        grid=(t // tm, ne // te),
        in_specs=[pl.BlockSpec((d, tm), lambda i, e: (0, i)),
                  pl.BlockSpec((te, d), lambda i, e: (e, 0)),
                  pl.BlockSpec((d, te), lambda i, e: (0, e)),
                  gate, gate, gate, gate,
                  pl.BlockSpec((PEER_HEADS, tm), lambda i, e: (0, i))],
        out_specs=pl.BlockSpec((d, tm), lambda i, e: (0, i)),
        out_shape=jax.ShapeDtypeStruct((d, t), F32),
        compiler_params=_cparams("arbitrary", "arbitrary"),
        name="peer_experts",
    )(h2t, u_tab, vt_tab, s1, s2, e1, e2, tau)


def _fin_kernel(ot_ref, x1_ref, ada_ref, y_ref):
    y_ref[...] = x1_ref[...] + ada_ref[0, 5:6, :] * ot_ref[...].T


def _final_residual(out_t, x1, ada_g, seq_len):
    t, d = x1.shape
    tm = 256
    per_seq = seq_len // tm
    return pl.pallas_call(
        _fin_kernel,
        grid=(t // tm,),
        in_specs=[pl.BlockSpec((d, tm), lambda i: (0, i)),
                  pl.BlockSpec((tm, d), lambda i: (i, 0)),
                  pl.BlockSpec((1, 6, d), lambda i: (i // per_seq, 0, 0))],
        out_specs=pl.BlockSpec((tm, d), lambda i: (i, 0)),
        out_shape=jax.ShapeDtypeStruct((t, d), F32),
        compiler_params=_cparams("arbitrary"),
        name="final_residual",
    )(out_t, x1, ada_g)


def _peer_gates(scores):
    s1, s2 = scores[0::2], scores[1::2]
    s1 = s1 - jnp.max(s1, axis=1, keepdims=True)
    s2 = s2 - jnp.max(s2, axis=1, keepdims=True)
    top1 = lax.top_k(jnp.moveaxis(s1, 1, -1), PEER_TOPK)[0]
    top2 = lax.top_k(jnp.moveaxis(s2, 1, -1), PEER_TOPK)[0]
    cand = (top1[..., :, None] + top2[..., None, :]).reshape(top1.shape[:-1] + (PEER_TOPK * PEER_TOPK,))
    best = lax.top_k(cand, PEER_TOPK)[0]
    tau = best[..., -1]
    zsum = jnp.sum(jnp.exp(best), axis=-1)
    e1 = jnp.exp(s1) / zsum[:, None, :]
    e2 = jnp.exp(s2)
    return s1, s2, e1, e2, tau


def _encode_group(x, ada_g, prm):
    b, L, d = x.shape
    (q1, q4, q16), (k1, k4, k16), (v1, v4, v16) = _qkv_proj(
        x, ada_g, prm["norm1_g"], prm["w_qkv"], prm["gq"], prm["gk"], prm["ones_bd"])
    att_parts = [
        _band_attention(q1, k1, v1, prm["bias"][0], b, 1),
        _band_attention(q4, k4, v4, prm["bias"][1], b, 4),
        _band_attention(q16, k16, v16, prm["bias"][2], b, 16),
    ]
    hy = _hy_proj(x, ada_g, prm["norm1_g"], prm["w_hy"])
    z, x0c = _hy_pre(hy, prm["short_w"], prm["short_b"])
    dft = _shifted_dft_matrix(L)
    taps = _hyena_filter_taps(L, *prm["filter"])
    spec = _filter_spectrum(dft, taps)
    yspec = _hy_forward(dft, z, spec)
    hyo = _hy_inverse(dft, yspec, z, x0c, prm["bias_d"])
    x1, h2 = _out_proj(att_parts, hyo, x, ada_g, prm["att_out_g"], prm["hyena_out_g"], prm["w_out"], prm["norm2_g"])
    t = b * L
    h2 = h2.reshape(t, d)
    scores = _peer_scores(h2, prm["w_query"], prm["sub_keys"])
    s1, s2, e1, e2, tau = _peer_gates(scores)
    out_t = _peer_experts(h2.T, prm["u_tab"], prm["vt_tab"], s1, s2, e1, e2, tau)
    y = _final_residual(out_t, x1.reshape(t, d), ada_g, L)
    return y.reshape(b, L, d)


def kernel(x_prompt, x_sample, c_prompt, c_sample, rel_bias, w_ada, b_ada, norm1_g, w_in, q_norm_g, k_norm_g, hyena_short_w, hyena_short_b, hyena_ffn_w1, hyena_ffn_b1, hyena_ffn_w2, hyena_ffn_b2, hyena_ffn_w3, hyena_ffn_b3, hyena_sin_freq, hyena_decay, hyena_bias_d, att_out_g, hyena_out_g, w_out, norm2_g, peer_w_query, peer_sub_keys, peer_u, peer_v):
    assert w_ada.shape[0] == 1, "single-layer problem"
    d = x_prompt.shape[-1]
    bp, bs = x_prompt.shape[0], x_sample.shape[0]
    att_w = att_out_g.shape[1]
    nheads = att_w // HEAD_DIM

    c_all = jnp.concatenate([c_prompt, c_sample], axis=0)
    pad = (-c_all.shape[0]) % 8
    ada = _ada(jnp.pad(c_all, ((0, pad), (0, 0))), w_ada[0], b_ada[0])
    ada = ada[:bp + bs].reshape(bp + bs, 6, d)

    w_in0 = w_in[0]
    lane = jnp.arange(LANES)
    prm = {
        "norm1_g": norm1_g[0].reshape(1, d),
        "w_qkv": w_in0[:, :3 * att_w].astype(BF16),
        "w_hy": w_in0[:, 3 * att_w:].astype(BF16),
        "gq": jnp.tile(q_norm_g[0], nheads).reshape(1, att_w),
        "gk": jnp.tile(k_norm_g[0], nheads).reshape(1, att_w),
        "ones_bd": (lane[:, None] // HEAD_DIM == lane[None, :] // HEAD_DIM).astype(BF16),
        "bias": [_band_bias(rel_bias, dil) for dil in DILATIONS],
        "short_w": hyena_short_w[0],
        "short_b": hyena_short_b[0],
        "filter": (hyena_ffn_w1[0], hyena_ffn_b1[0], hyena_ffn_w2[0], hyena_ffn_b2[0],
                   hyena_ffn_w3[0], hyena_ffn_b3[0], hyena_sin_freq[0], hyena_decay[0]),
        "bias_d": hyena_bias_d[0],
        "att_out_g": att_out_g[0].reshape(1, -1),
        "hyena_out_g": hyena_out_g[0].reshape(1, -1),
        "w_out": w_out[0].astype(BF16),
        "norm2_g": norm2_g[0].reshape(1, d),
        "w_query": peer_w_query[0].astype(BF16),
        "sub_keys": peer_sub_keys[0].reshape(2 * PEER_HEADS, PEER_KEYS, -1).astype(BF16),
        "u_tab": peer_u[0].astype(BF16),
        "vt_tab": peer_v[0].astype(BF16).T,
    }
    y_prompt = _encode_group(x_prompt, ada[:bp], prm)
    y_sample = _encode_group(x_sample, ada[bp:], prm)
    return (y_prompt, y_sample)
```

```python
import functools
import math

import jax
import jax.numpy as jnp
from jax import lax
from jax.experimental import pallas as pl
from jax.experimental.pallas import tpu as pltpu

F32 = jnp.float32
BF16 = jnp.bfloat16
HIGHEST = lax.Precision.HIGHEST

HEAD_DIM = 64
RMS_EPS = 1e-6
MASKED = -1e30
UNSELECTED = 3e38
DILATIONS = (1, 4, 16)
HALF_WINDOW = 64
MAX_DILATION = 16
REL_BUCKETS = 32
REL_MAX_DISTANCE = 1024
HYENA_EMB_DIM = 33
PEER_KEYS = 128
PEER_HEADS = 8
PEER_TOPK = 16

LANES = 128
V7X_VMEM_LIMIT_BYTES = 58 * 1024 * 1024


def _cparams(*sem):
    return pltpu.CompilerParams(dimension_semantics=sem, vmem_limit_bytes=V7X_VMEM_LIMIT_BYTES)


def _dot(a, b, **kw):
    return jnp.dot(a, b, preferred_element_type=F32, **kw)


def _rms(x):
    return x * lax.rsqrt(jnp.mean(x * x, axis=-1, keepdims=True) + RMS_EPS)


def _modulated_norm(x, gain, scale, shift):
    return (_rms(x) * gain) * (1.0 + scale) + shift


def _ada_kernel(c_ref, w_ref, b_ref, o_ref):
    c = c_ref[...]
    a = c / (1.0 + jnp.exp(-c))
    o_ref[...] = _dot(a, w_ref[...], precision=HIGHEST) + b_ref[...]


def _ada(c_all, w_ada, b_ada):
    nb, d = c_all.shape
    n = w_ada.shape[1]
    tn = 768
    return pl.pallas_call(
        _ada_kernel,
        grid=(n // tn,),
        in_specs=[pl.BlockSpec((nb, d), lambda j: (0, 0)),
                  pl.BlockSpec((d, tn), lambda j: (0, j)),
                  pl.BlockSpec((1, tn), lambda j: (0, j))],
        out_specs=pl.BlockSpec((nb, tn), lambda j: (0, j)),
        out_shape=jax.ShapeDtypeStruct((nb, n), F32),
        compiler_params=_cparams("arbitrary"),
        name="ada",
    )(c_all, w_ada, b_ada.reshape(1, n))


def _head_rms(y, gain, ones_bd):
    outs = []
    for c in range(y.shape[1] // LANES):
        yc = y[:, c * LANES:(c + 1) * LANES]
        sq = yc * yc
        hi = sq.astype(BF16)
        lo = (sq - hi.astype(F32)).astype(BF16)
        ss = _dot(hi, ones_bd) + _dot(lo, ones_bd)
        outs.append(yc * lax.rsqrt(ss * (1.0 / HEAD_DIM) + RMS_EPS))
    return jnp.concatenate(outs, axis=1) * gain


def _qkv_kernel(x_ref, ada_ref, g_ref, w_ref, gq_ref, gk_ref, bd_ref, *outs):
    h = _modulated_norm(x_ref[0], g_ref[...], ada_ref[0, 1:2, :], ada_ref[0, 0:1, :]).astype(BF16)
    y = _dot(h, w_ref[...])
    w = y.shape[1] // 3
    bd = bd_ref[...]
    q = _head_rms(y[:, :w], gq_ref[...], bd) * (HEAD_DIM ** -0.5)
    k = _head_rms(y[:, w:2 * w], gk_ref[...], bd)
    v = y[:, 2 * w:]
    for j, t in enumerate((q, k, v)):
        tb = t.astype(BF16)
        outs[3 * j][0] = tb
        outs[3 * j + 1][0, 0] = tb
        outs[3 * j + 2][0, 0] = tb


def _qkv_proj(x, ada_g, norm_g, w_qkv, gq, gk, ones_bd):
    b, L, d = x.shape
    r = MAX_DILATION
    m = L // r
    w = w_qkv.shape[1] // 3
    xv = x.reshape(b, m, r * d)
    shapes, specs = [], []
    for _ in range(3):
        shapes += [jax.ShapeDtypeStruct((b, m, r * w), BF16),
                   jax.ShapeDtypeStruct((b, 4, m, 4 * w), BF16),
                   jax.ShapeDtypeStruct((b, r, m, w), BF16)]
        specs += [pl.BlockSpec((1, m, w), lambda bi, ri: (bi, 0, ri)),
                  pl.BlockSpec((1, 1, m, w), lambda bi, ri: (bi, ri % 4, 0, ri // 4)),
                  pl.BlockSpec((1, 1, m, w), lambda bi, ri: (bi, ri, 0, 0))]
    outs = pl.pallas_call(
        _qkv_kernel,
        grid=(b, r),
        in_specs=[pl.BlockSpec((1, m, d), lambda bi, ri: (bi, 0, ri)),
                  pl.BlockSpec((1, 6, d), lambda bi, ri: (bi, 0, 0)),
                  pl.BlockSpec((1, d), lambda bi, ri: (0, 0)),
                  pl.BlockSpec((d, 3 * w), lambda bi, ri: (0, 0)),
                  pl.BlockSpec((1, w), lambda bi, ri: (0, 0)),
                  pl.BlockSpec((1, w), lambda bi, ri: (0, 0)),
                  pl.BlockSpec((LANES, LANES), lambda bi, ri: (0, 0))],
        out_specs=specs,
        out_shape=shapes,
        compiler_params=_cparams("arbitrary", "arbitrary"),
        name="qkv_proj",
    )(xv, ada_g, norm_g, w_qkv, gq, gk, ones_bd)
    res = []
    for j in range(3):
        nat, p4, p16 = outs[3 * j:3 * j + 3]
        res.append((nat.reshape(b, L, w), p4.reshape(b * 4, L // 4, w), p16.reshape(b * r, m, w)))
    return res


def _hyproj_kernel(x_ref, ada_ref, g_ref, w_ref, o_ref):
    h = _modulated_norm(x_ref[0], g_ref[...], ada_ref[0, 1:2, :], ada_ref[0, 0:1, :]).astype(BF16)
    o_ref[0] = _dot(h, w_ref[...]).astype(BF16)


def _hy_proj(x, ada_g, norm_g, w_hy):
    b, L, d = x.shape
    n = w_hy.shape[1]
    tm = 512
    return pl.pallas_call(
        _hyproj_kernel,
        grid=(b, L // tm),
        in_specs=[pl.BlockSpec((1, tm, d), lambda bi, i: (bi, i, 0)),
                  pl.BlockSpec((1, 6, d), lambda bi, i: (bi, 0, 0)),
                  pl.BlockSpec((1, d), lambda bi, i: (0, 0)),
                  pl.BlockSpec((d, n), lambda bi, i: (0, 0))],
        out_specs=pl.BlockSpec((1, tm, n), lambda bi, i: (bi, i, 0)),
        out_shape=jax.ShapeDtypeStruct((b, L, n), BF16),
        compiler_params=_cparams("arbitrary", "arbitrary"),
        name="hy_proj",
    )(x, ada_g, norm_g, w_hy)


def _att_kernel(q_ref, kp_ref, kc_ref, kn_ref, vp_ref, vc_ref, vn_ref, bias_ref, o_ref, l_ref):
    width = q_ref.shape[2]
    first = lax.broadcasted_iota(jnp.int32, (1, LANES), 1) < HEAD_DIM
    for hp in range(width // LANES):
        sl = slice(hp * LANES, (hp + 1) * LANES)
        q = q_ref[0, :, sl]
        k = jnp.concatenate([kp_ref[0, :, sl], kc_ref[0, :, sl], kn_ref[0, :, sl]], axis=0)
        v = jnp.concatenate([vp_ref[0, :, sl], vc_ref[0, :, sl], vn_ref[0, :, sl]], axis=0)
        o_pair, l_pair = [], []
        for hh in range(2):
            sel = first if hh == 0 else jnp.logical_not(first)
            qh = jnp.where(sel, q, jnp.zeros_like(q))
            s = lax.dot_general(qh, k, (((1,), (1,)), ((), ())), preferred_element_type=F32)
            s = s + bias_ref[0, 2 * hp + hh]
            mx = jnp.max(s, axis=-1, keepdims=True)
            p = jnp.exp(s - mx)
            den = jnp.sum(p, axis=-1, keepdims=True)
            o_pair.append(_dot(p.astype(BF16), v) / den)
            l_pair.append(mx + jnp.log(den))
        o_ref[0, :, sl] = jnp.where(first, o_pair[0], o_pair[1])
        l_ref[0, :, sl] = jnp.where(first, l_pair[0], l_pair[1])


def _t5_bucket_of(rel):
    half = REL_BUCKETS // 2
    exact = half // 2
    n = jnp.abs(rel)
    nf = jnp.maximum(n, 1).astype(F32)
    large = exact + (jnp.log(nf / exact) / math.log(REL_MAX_DISTANCE / exact) * (half - exact)).astype(jnp.int32)
    large = jnp.minimum(large, half - 1)
    return jnp.where(rel > 0, half, 0) + jnp.where(n < exact, n, large)


def _band_bias(rel_bias, dilation):
    tq, tk, hw = 2 * HALF_WINDOW, 4 * HALF_WINDOW, HALF_WINDOW
    i = jnp.arange(tq)[:, None]
    j = jnp.arange(tk)[None, :]
    d = j - hw - i
    bias = rel_bias.astype(F32)[_t5_bucket_of(d * dilation)].transpose(2, 0, 1)
    band = jnp.abs(d) <= hw
    out = []
    for var in range(4):
        ok = band
        if var & 1:
            ok = ok & (j >= hw)
        if var & 2:
            ok = ok & (j < tk - hw)
        out.append(jnp.where(ok[None], bias, MASKED))
    return jnp.stack(out)


def _band_attention(q, k, v, bias, batch, dilation):
    bp, m, w = q.shape
    tq, hw = 2 * HALF_WINDOW, HALF_WINDOW
    nq = m // tq
    nh = w // HEAD_DIM

    def var_idx(qi):
        return (qi == 0).astype(jnp.int32) + 2 * (qi == nq - 1).astype(jnp.int32)

    prev = lambda b, qi: (b, jnp.maximum(2 * qi - 1, 0), 0)
    cur = lambda b, qi: (b, qi, 0)
    nxt = lambda b, qi: (b, jnp.minimum(2 * qi + 2, 2 * nq - 1), 0)
    out_map = lambda b, qi: (b // dilation, qi, b % dilation)
    o, l = pl.pallas_call(
        _att_kernel,
        grid=(bp, nq),
        in_specs=[pl.BlockSpec((1, tq, w), cur),
                  pl.BlockSpec((1, hw, w), prev), pl.BlockSpec((1, tq, w), cur), pl.BlockSpec((1, hw, w), nxt),
                  pl.BlockSpec((1, hw, w), prev), pl.BlockSpec((1, tq, w), cur), pl.BlockSpec((1, hw, w), nxt),
                  pl.BlockSpec((1, nh, tq, 2 * tq), lambda b, qi: (var_idx(qi), 0, 0, 0))],
        out_specs=[pl.BlockSpec((1, tq, w), out_map), pl.BlockSpec((1, tq, w), out_map)],
        out_shape=[jax.ShapeDtypeStruct((batch, m, dilation * w), F32)] * 2,
        compiler_params=_cparams("arbitrary", "arbitrary"),
        name=f"band_att_d{dilation}",
    )(q, k, k, k, v, v, v, bias)
    L = m * dilation
    return o.reshape(batch, L, w), l.reshape(batch, L, w)


def _hy_pre_kernel(x0_ref, x1_ref, v_ref, w_ref, b_ref, z_ref, x0c_ref):
    L, tn = x0_ref.shape[1], x0_ref.shape[2]
    row = lax.broadcasted_iota(jnp.int32, (L, tn), 0)

    def conv(ref, s):
        u = ref[0].astype(F32)
        um = jnp.where(row == 0, 0.0, pltpu.roll(u, 1, 0))
        up = jnp.where(row == L - 1, 0.0, pltpu.roll(u, L - 1, 0))
        return um * w_ref[0, s:s + 1, :] + u * w_ref[1, s:s + 1, :] + up * w_ref[2, s:s + 1, :] + b_ref[s:s + 1, :]

    x0c_ref[0] = conv(x0_ref, 0).astype(BF16)
    z_ref[0] = (conv(v_ref, 2) * conv(x1_ref, 1)).astype(BF16)


def _hy_pre(hy, short_w, short_b):
    b, L, n3 = hy.shape
    c = n3 // 3
    tn = LANES
    nt = c // tn
    w3 = short_w.reshape(3, 3, c)
    b3 = short_b.reshape(3, c)
    stream = lambda s: pl.BlockSpec((1, L, tn), lambda bi, n: (bi, 0, s * nt + n))
    return pl.pallas_call(
        _hy_pre_kernel,
        grid=(b, nt),
        in_specs=[stream(0), stream(1), stream(2),
                  pl.BlockSpec((3, 3, tn), lambda bi, n: (0, 0, n)),
                  pl.BlockSpec((3, tn), lambda bi, n: (0, n))],
        out_specs=[pl.BlockSpec((1, L, tn), lambda bi, n: (bi, 0, n))] * 2,
        out_shape=[jax.ShapeDtypeStruct((b, L, c), BF16)] * 2,
        compiler_params=_cparams("arbitrary", "arbitrary"),
        name="hy_pre",
    )(hy, hy, hy, w3, b3)


def _filt_kernel(z_ref, w1_ref, b1_ref, w2_ref, b2_ref, w3_ref, b3_ref, sf_ref, dec_ref, o_ref):
    z = z_ref[...]
    tl = z.shape[0]
    c = dec_ref.shape[1]
    h = jnp.sin(sf_ref[0:1, :] * (_dot(z, w1_ref[...], precision=HIGHEST) + b1_ref[...]))
    h = jnp.sin(sf_ref[1:2, :] * (_dot(h, w2_ref[...], precision=HIGHEST) + b2_ref[...]))
    hf = _dot(h, w3_ref[...], precision=HIGHEST) + b3_ref[...]
    t = z[:, 0:1]
    dec = jnp.abs(dec_ref[...])
    fwd = hf[:, :c] * jnp.exp(-t * dec[0:1, :])
    bwd = hf[:, c:] * jnp.exp(-t * dec[1:2, :])
    row = pl.program_id(0) * tl + lax.broadcasted_iota(jnp.int32, (tl, c), 0)
    o_ref[0] = fwd.astype(BF16)
    o_ref[1] = jnp.where(row == 0, 0.0, bwd).astype(BF16)


def _position_embedding(L):
    t = jnp.linspace(0.0, 1.0, L, dtype=F32)[:, None]
    bands = (HYENA_EMB_DIM - 1) // 2
    freqs = jnp.linspace(1e-4, bands - 1, bands, dtype=F32)[None, :]
    wpos = 2.0 * math.pi * jnp.arange(L, dtype=F32)[:, None] / L
    z = jnp.concatenate([t, jnp.cos(freqs * wpos), -jnp.sin(freqs * wpos)], axis=-1)
    return jnp.pad(z, ((0, 0), (0, LANES - HYENA_EMB_DIM)))


def _hyena_filter_taps(L, w1, b1, w2, b2, w3, b3, sin_freq, decay):
    fw = w1.shape[1]
    c = decay.shape[1]
    tl = 512
    zemb = _position_embedding(L)
    w1p = jnp.pad(w1, ((0, LANES - HYENA_EMB_DIM), (0, 0)))
    full = lambda shape: pl.BlockSpec(shape, lambda i: (0,) * len(shape))
    return pl.pallas_call(
        _filt_kernel,
        grid=(L // tl,),
        in_specs=[pl.BlockSpec((tl, LANES), lambda i: (i, 0)),
                  full((LANES, fw)), full((1, fw)), full((fw, fw)), full((1, fw)),
                  full((fw, 2 * c)), full((1, 2 * c)), full((2, fw)), full((2, c))],
        out_specs=pl.BlockSpec((2, tl, c), lambda i: (0, i, 0)),
        out_shape=jax.ShapeDtypeStruct((2, L, c), BF16),
        compiler_params=_cparams("arbitrary"),
        name="hy_filter_taps",
    )(zemb, w1p, b1.reshape(1, fw), w2, b2.reshape(1, fw), w3, b3.reshape(1, 2 * c), sin_freq, decay)


def _shifted_dft_matrix(L):
    odd = 2 * jnp.arange(L, dtype=jnp.int32) + 1
    phase = (odd[:, None] * odd[None, :]) % (8 * L)
    ang = phase.astype(F32) * (2.0 * math.pi / (8 * L))
    return jnp.stack([jnp.cos(ang), jnp.sin(ang)]).astype(BF16)


def _filt_dft_kernel(m_ref, r_ref, ct_ref, st_ref, h_ref):
    a = _dot(m_ref[0], r_ref[0])
    b = _dot(m_ref[1], r_ref[0])
    c = _dot(m_ref[0], r_ref[1])
    d = _dot(m_ref[1], r_ref[1])
    ct, st = ct_ref[...], st_ref[...]
    h_ref[0] = ct * (a + c) + st * (b + d)
    h_ref[1] = st * (a - c) + ct * (d - b)


def _filter_spectrum(dft, taps):
    _, L, c = taps.shape
    tk, tn = 256, min(256, c)
    theta = (2 * jnp.arange(L, dtype=F32) + 1) * (math.pi / (4 * L))
    ct = jnp.cos(theta)[:, None]
    st = jnp.sin(theta)[:, None]
    return pl.pallas_call(
        _filt_dft_kernel,
        grid=(c // tn, L // tk),
        in_specs=[pl.BlockSpec((2, tk, L), lambda n, k: (0, k, 0)),
                  pl.BlockSpec((2, L, tn), lambda n, k: (0, 0, n)),
                  pl.BlockSpec((tk, 1), lambda n, k: (k, 0)),
                  pl.BlockSpec((tk, 1), lambda n, k: (k, 0))],
        out_specs=pl.BlockSpec((2, tk, tn), lambda n, k: (0, k, n)),
        out_shape=jax.ShapeDtypeStruct((2, L, c), F32),
        compiler_params=_cparams("arbitrary", "arbitrary"),
        name="hy_filter_dft",
    )(dft, taps, ct, st)


def _hy_fwd_kernel(m_ref, z_ref, h_ref, y_ref):
    zc = _dot(m_ref[0], z_ref[0])
    zs = _dot(m_ref[1], z_ref[0])
    hr, hi = h_ref[0], h_ref[1]
    y_ref[0, 0] = (zc * hr + zs * hi).astype(BF16)
    y_ref[0, 1] = (zc * hi - zs * hr).astype(BF16)


def _hy_forward(dft, z, spec):
    b, L, c = z.shape
    tk, tn = 256, min(512, c)
    return pl.pallas_call(
        _hy_fwd_kernel,
        grid=(b, c // tn, L // tk),
        in_specs=[pl.BlockSpec((2, tk, L), lambda bi, n, k: (0, k, 0)),
                  pl.BlockSpec((1, L, tn), lambda bi, n, k: (bi, 0, n)),
                  pl.BlockSpec((2, tk, tn), lambda bi, n, k: (0, k, n))],
        out_specs=pl.BlockSpec((1, 2, tk, tn), lambda bi, n, k: (bi, 0, k, n)),
        out_shape=jax.ShapeDtypeStruct((b, 2, L, c), BF16),
        compiler_params=_cparams("arbitrary", "arbitrary", "arbitrary"),
        name="hy_dft_fwd",
    )(dft, z, spec)


def _hy_inv_kernel(m_ref, y_ref, z_ref, x0_ref, bd_ref, o_ref, *, scale):
    conv = _dot(m_ref[0], y_ref[0, 0]) - _dot(m_ref[1], y_ref[0, 1])
    y = conv * scale + z_ref[0].astype(F32) * bd_ref[...]
    o_ref[0] = x0_ref[0].astype(F32) * y


def _hy_inverse(dft, yspec, z, x0c, bias_d):
    b, L, c = z.shape
    tt, tn = 256, min(512, c)
    return pl.pallas_call(
        functools.partial(_hy_inv_kernel, scale=1.0 / L),
        grid=(b, c // tn, L // tt),
        in_specs=[pl.BlockSpec((2, tt, L), lambda bi, n, t: (0, t, 0)),
                  pl.BlockSpec((1, 2, L, tn), lambda bi, n, t: (bi, 0, 0, n)),
                  pl.BlockSpec((1, tt, tn), lambda bi, n, t: (bi, t, n)),
                  pl.BlockSpec((1, tt, tn), lambda bi, n, t: (bi, t, n)),
                  pl.BlockSpec((1, tn), lambda bi, n, t: (0, n))],
        out_specs=pl.BlockSpec((1, tt, tn), lambda bi, n, t: (bi, t, n)),
        out_shape=jax.ShapeDtypeStruct((b, L, c), F32),
        compiler_params=_cparams("arbitrary", "arbitrary", "arbitrary"),
        name="hy_dft_inv",
    )(dft, yspec, z, x0c, bias_d.reshape(1, c))


def _out_kernel(o1_ref, o2_ref, o3_ref, l1_ref, l2_ref, l3_ref, hy_ref, x_ref, ada_ref,
                ga_ref, gh_ref, w_ref, g2_ref, x1_ref, h2_ref):
    la, lb, lc = l1_ref[0], l2_ref[0], l3_ref[0]
    lm = jnp.maximum(jnp.maximum(la, lb), lc)
    wa, wb, wc = jnp.exp(la - lm), jnp.exp(lb - lm), jnp.exp(lc - lm)
    att = (wa * o1_ref[0] + wb * o2_ref[0] + wc * o3_ref[0]) / (wa + wb + wc)
    wa_ = att.shape[1]
    an = (_rms(att) * ga_ref[...]).astype(BF16)
    hn = (_rms(hy_ref[0]) * gh_ref[...]).astype(BF16)
    res = _dot(an, w_ref[:wa_, :]) + _dot(hn, w_ref[wa_:, :])
    x1 = x_ref[0] + ada_ref[0, 2:3, :] * res
    x1_ref[0] = x1
    h2_ref[0] = _modulated_norm(x1, g2_ref[...], ada_ref[0, 4:5, :], ada_ref[0, 3:4, :]).astype(BF16)


def _out_proj(att_parts, hyo, x, ada_g, att_g, hy_g, w_out, norm2_g):
    b, L, d = x.shape
    wa = hyo.shape[2]
    tm = 256
    tok = lambda width: pl.BlockSpec((1, tm, width), lambda bi, i: (bi, i, 0))
    full = lambda shape: pl.BlockSpec(shape, lambda bi, i: (0,) * len(shape))
    (o1, l1), (o2, l2), (o3, l3) = att_parts
    return pl.pallas_call(
        _out_kernel,
        grid=(b, L // tm),
        in_specs=[tok(wa)] * 7 + [tok(d), pl.BlockSpec((1, 6, d), lambda bi, i: (bi, 0, 0)),
                                  full((1, wa)), full((1, wa)), full((d, d)), full((1, d))],
        out_specs=[tok(d), tok(d)],
        out_shape=[jax.ShapeDtypeStruct((b, L, d), F32), jax.ShapeDtypeStruct((b, L, d), BF16)],
        compiler_params=_cparams("arbitrary", "arbitrary"),
        name="out_proj",
    )(o1, o2, o3, l1, l2, l3, hyo, x, ada_g, att_g, hy_g, w_out, norm2_g)


def _peer_q_kernel(h_ref, w_ref, sk_ref, s_ref):
    q = _dot(h_ref[...], w_ref[...]).astype(BF16)
    for c in range(sk_ref.shape[0]):
        qc = q[:, c * PEER_KEYS:(c + 1) * PEER_KEYS]
        s_ref[c] = lax.dot_general(sk_ref[c], qc, (((1,), (1,)), ((), ())), preferred_element_type=F32)


def _peer_scores(h2, w_query, sub_keys):
    t, d = h2.shape
    nc, nk, qd = sub_keys.shape
    tm = 512
    return pl.pallas_call(
        _peer_q_kernel,
        grid=(t // tm,),
        in_specs=[pl.BlockSpec((tm, d), lambda i: (i, 0)),
                  pl.BlockSpec((d, nc * qd), lambda i: (0, 0)),
                  pl.BlockSpec((nc, nk, qd), lambda i: (0, 0, 0))],
        out_specs=pl.BlockSpec((nc, nk, tm), lambda i: (0, 0, i)),
        out_shape=jax.ShapeDtypeStruct((nc, nk, t), F32),
        compiler_params=_cparams("arbitrary"),
        name="peer_scores",
    )(h2, w_query, sub_keys)


def _gelu_tanh(x):
    return x * (0.5 * (1.0 + jnp.tanh(math.sqrt(2.0 / math.pi) * (x + 0.044715 * (x * x * x)))))


def _peer_main_kernel(ht_ref, u_ref, vt_ref, th_ref, s2_ref, e1_ref, e2_ref, o_ref, at_ref, pt_ref):
    e = pl.program_id(1)
    te = u_ref.shape[0]
    ni = te // PEER_KEYS

    @pl.when(e == 0)
    def _():
        o_ref[...] = jnp.zeros_like(o_ref)

    at_ref[...] = _dot(u_ref[...], ht_ref[...])
    for ii in range(ni):
        idx = e * ni + ii
        rows = slice(ii * PEER_KEYS, (ii + 1) * PEER_KEYS)
        for c in range(ht_ref.shape[1] // LANES):
            cols = slice(c * LANES, (c + 1) * LANES)
            w = jnp.zeros((PEER_KEYS, LANES), F32)
            for h in range(PEER_HEADS):
                chosen = s2_ref[h, c] >= th_ref[h, c, pl.ds(idx, 1), :]
                gate = e1_ref[h, c, pl.ds(idx, 1), :] * e2_ref[h, c]
                w = w + jnp.where(chosen, gate, 0.0)
            pt_ref[rows, cols] = (_gelu_tanh(at_ref[rows, cols]) * w).astype(BF16)
    o_ref[...] += _dot(vt_ref[...], pt_ref[...])


def _peer_experts(h2t, u_tab, vt_tab, theta, s2m, e1, e2):
    d, t = h2t.shape
    ne = u_tab.shape[0]
    tm, te = 512, 512
    gate = pl.BlockSpec((PEER_HEADS, tm // LANES, PEER_KEYS, LANES), lambda i, e: (0, i, 0, 0))
    return pl.pallas_call(
        _peer_main_kernel,
        grid=(t // tm, ne // te),
        in_specs=[pl.BlockSpec((d, tm), lambda i, e: (0, i)),
                  pl.BlockSpec((te, d), lambda i, e: (e, 0)),
                  pl.BlockSpec((d, te), lambda i, e: (0, e)),
                  gate, gate, gate, gate],
        out_specs=pl.BlockSpec((d, tm), lambda i, e: (0, i)),
        out_shape=jax.ShapeDtypeStruct((d, t), F32),
        scratch_shapes=[pltpu.VMEM((te, tm), F32), pltpu.VMEM((te, tm), BF16)],
        compiler_params=_cparams("arbitrary", "arbitrary"),
        name="peer_experts",
    )(h2t, u_tab, vt_tab, theta, s2m, e1, e2)


def _fin_kernel(ot_ref, x1_ref, ada_ref, y_ref):
    y_ref[...] = x1_ref[...] + ada_ref[0, 5:6, :] * ot_ref[...].T


def _final_residual(out_t, x1, ada_g, seq_len):
    t, d = x1.shape
    tm = 256
    per_seq = seq_len // tm
    return pl.pallas_call(
        _fin_kernel,
        grid=(t // tm,),
        in_specs=[pl.BlockSpec((d, tm), lambda i: (0, i)),
                  pl.BlockSpec((tm, d), lambda i: (i, 0)),
                  pl.BlockSpec((1, 6, d), lambda i: (i // per_seq, 0, 0))],
        out_specs=pl.BlockSpec((tm, d), lambda i: (i, 0)),
        out_shape=jax.ShapeDtypeStruct((t, d), F32),
        compiler_params=_cparams("arbitrary"),
        name="final_residual",
    )(out_t, x1, ada_g)


def _extract_top(work_ref, vals_ref, count, idx_ref=None):
    n, tm = work_ref.shape
    rows = lax.broadcasted_iota(jnp.int32, (n, tm), 0)

    def body(r, carry):
        s = work_ref[...]
        m = jnp.max(s, axis=0, keepdims=True)
        idx = jnp.min(jnp.where(s == m, rows, n), axis=0, keepdims=True)
        work_ref[...] = jnp.where(rows == idx, MASKED, s)
        vals_ref[pl.ds(r, 1), :] = m
        if idx_ref is not None:
            idx_ref[pl.ds(r, 1), :] = idx
        return carry

    lax.fori_loop(0, count, body, 0)


_CAND_SECOND = 8
_CAND_ROWS = PEER_TOPK + 7 * _CAND_SECOND + (PEER_TOPK - 8)


def _peer_gate_kernel(s_ref, th_ref, s2m_ref, e1_ref, e2_ref, work_ref, v1_ref, v2_ref,
                      cand_ref, best_ref, idx_ref):
    k = PEER_TOPK
    s1 = s_ref[0]
    s2 = s_ref[1]
    work_ref[...] = s1
    _extract_top(work_ref, v1_ref, k, idx_ref)
    in1 = work_ref[...] == MASKED
    work_ref[...] = s2
    _extract_top(work_ref, v2_ref, k)
    in2 = work_ref[...] == MASKED

    v1 = v1_ref[...]
    v2 = v2_ref[...]
    cand_ref[0:k, :] = v1[0:1] + v2
    brow = lax.broadcasted_iota(jnp.int32, (_CAND_SECOND, v2.shape[1]), 0)
    for a in range(1, 8):
        blk = v1[a:a + 1] + v2[0:_CAND_SECOND]
        lo = k + (a - 1) * _CAND_SECOND
        cand_ref[lo:lo + _CAND_SECOND, :] = jnp.where(brow < k // (a + 1), blk, MASKED)
    cand_ref[k + 7 * _CAND_SECOND:, :] = v1[8:k] + v2[0:1]
    _extract_top(cand_ref, best_ref, k)

    best = best_ref[...]
    zsum = jnp.sum(jnp.exp(best - best[0:1]), axis=0, keepdims=True)
    tau = best[k - 1:k]

    rows = lax.broadcasted_iota(jnp.int32, s1.shape, 0)
    theta = jnp.full(s1.shape, UNSELECTED, F32)
    for a in range(k):
        th_a = jnp.min(jnp.where(v1[a:a + 1] + v2 >= tau, v2, UNSELECTED), axis=0, keepdims=True)
        theta = jnp.where(rows == idx_ref[a:a + 1, :], th_a, theta)

    s2m = jnp.where(in2, s2, MASKED)
    e1 = jnp.where(in1, jnp.exp(s1 - v1[0:1]) / zsum, 0.0)
    e2 = jnp.where(in2, jnp.exp(s2 - v2[0:1]), 0.0)
    for c in range(s1.shape[1] // LANES):
        cols = slice(c * LANES, (c + 1) * LANES)
        th_ref[0, c] = theta[:, cols]
        s2m_ref[0, c] = s2m[:, cols]
        e1_ref[0, c] = e1[:, cols]
        e2_ref[0, c] = e2[:, cols]


def _peer_gates(scores):
    nc, nk, t = scores.shape
    nh = nc // 2
    tm = 512
    ncl = tm // LANES
    big = pl.BlockSpec((1, ncl, nk, LANES), lambda i, h: (h, i, 0, 0))
    big_shape = jax.ShapeDtypeStruct((nh, t // LANES, nk, LANES), F32)
    return pl.pallas_call(
        _peer_gate_kernel,
        grid=(t // tm, nh),
        in_specs=[pl.BlockSpec((2, nk, tm), lambda i, h: (h, 0, i))],
        out_specs=[big, big, big, big],
        out_shape=[big_shape] * 4,
        scratch_shapes=[pltpu.VMEM((nk, tm), F32), pltpu.VMEM((PEER_TOPK, tm), F32),
                        pltpu.VMEM((PEER_TOPK, tm), F32), pltpu.VMEM((_CAND_ROWS, tm), F32),
                        pltpu.VMEM((PEER_TOPK, tm), F32), pltpu.VMEM((PEER_TOPK, tm), jnp.int32)],
        compiler_params=_cparams("arbitrary", "arbitrary"),
        name="peer_gates",
    )(scores)


def _encode_group(x, ada_g, prm):
    b, L, d = x.shape
    (q1, q4, q16), (k1, k4, k16), (v1, v4, v16) = _qkv_proj(
        x, ada_g, prm["norm1_g"], prm["w_qkv"], prm["gq"], prm["gk"], prm["ones_bd"])
    att_parts = [
        _band_attention(q1, k1, v1, prm["bias"][0], b, 1),
        _band_attention(q4, k4, v4, prm["bias"][1], b, 4),
        _band_attention(q16, k16, v16, prm["bias"][2], b, 16),
    ]
    hy = _hy_proj(x, ada_g, prm["norm1_g"], prm["w_hy"])
    z, x0c = _hy_pre(hy, prm["short_w"], prm["short_b"])
    dft = _shifted_dft_matrix(L)
    taps = _hyena_filter_taps(L, *prm["filter"])
    spec = _filter_spectrum(dft, taps)
    yspec = _hy_forward(dft, z, spec)
    hyo = _hy_inverse(dft, yspec, z, x0c, prm["bias_d"])
    x1, h2 = _out_proj(att_parts, hyo, x, ada_g, prm["att_out_g"], prm["hyena_out_g"], prm["w_out"], prm["norm2_g"])
    t = b * L
    h2 = h2.reshape(t, d)
    scores = _peer_scores(h2, prm["w_query"], prm["sub_keys"])
    theta, s2m, e1, e2 = _peer_gates(scores)
    out_t = _peer_experts(h2.T, prm["u_tab"], prm["vt_tab"], theta, s2m, e1, e2)
    y = _final_residual(out_t, x1.reshape(t, d), ada_g, L)
    return y.reshape(b, L, d)


def kernel(x_prompt, x_sample, c_prompt, c_sample, rel_bias, w_ada, b_ada, norm1_g, w_in, q_norm_g, k_norm_g, hyena_short_w, hyena_short_b, hyena_ffn_w1, hyena_ffn_b1, hyena_ffn_w2, hyena_ffn_b2, hyena_ffn_w3, hyena_ffn_b3, hyena_sin_freq, hyena_decay, hyena_bias_d, att_out_g, hyena_out_g, w_out, norm2_g, peer_w_query, peer_sub_keys, peer_u, peer_v):
    assert w_ada.shape[0] == 1, "single-layer problem"
    d = x_prompt.shape[-1]
    bp, bs = x_prompt.shape[0], x_sample.shape[0]
    att_w = att_out_g.shape[1]
    nheads = att_w // HEAD_DIM

    c_all = jnp.concatenate([c_prompt, c_sample], axis=0)
    pad = (-c_all.shape[0]) % 8
    ada = _ada(jnp.pad(c_all, ((0, pad), (0, 0))), w_ada[0], b_ada[0])
    ada = ada[:bp + bs].reshape(bp + bs, 6, d)

    w_in0 = w_in[0]
    lane = jnp.arange(LANES)
    prm = {
        "norm1_g": norm1_g[0].reshape(1, d),
        "w_qkv": w_in0[:, :3 * att_w].astype(BF16),
        "w_hy": w_in0[:, 3 * att_w:].astype(BF16),
        "gq": jnp.tile(q_norm_g[0], nheads).reshape(1, att_w),
        "gk": jnp.tile(k_norm_g[0], nheads).reshape(1, att_w),
        "ones_bd": (lane[:, None] // HEAD_DIM == lane[None, :] // HEAD_DIM).astype(BF16),
        "bias": [_band_bias(rel_bias, dil) for dil in DILATIONS],
        "short_w": hyena_short_w[0],
        "short_b": hyena_short_b[0],
        "filter": (hyena_ffn_w1[0], hyena_ffn_b1[0], hyena_ffn_w2[0], hyena_ffn_b2[0],
                   hyena_ffn_w3[0], hyena_ffn_b3[0], hyena_sin_freq[0], hyena_decay[0]),
        "bias_d": hyena_bias_d[0],
        "att_out_g": att_out_g[0].reshape(1, -1),
        "hyena_out_g": hyena_out_g[0].reshape(1, -1),
        "w_out": w_out[0].astype(BF16),
        "norm2_g": norm2_g[0].reshape(1, d),
        "w_query": peer_w_query[0].astype(BF16),
        "sub_keys": peer_sub_keys[0].reshape(2 * PEER_HEADS, PEER_KEYS, -1).astype(BF16),
        "u_tab": peer_u[0].astype(BF16),
        "vt_tab": peer_v[0].astype(BF16).T,
    }
    y_prompt = _encode_group(x_prompt, ada[:bp], prm)
    y_sample = _encode_group(x_sample, ada[bp:], prm)
    return (y_prompt, y_sample)
```

```python
import functools
import math

import jax
import jax.numpy as jnp
from jax import lax
from jax.experimental import pallas as pl
from jax.experimental.pallas import tpu as pltpu

F32 = jnp.float32
BF16 = jnp.bfloat16
HIGHEST = lax.Precision.HIGHEST

HEAD_DIM = 64
RMS_EPS = 1e-6
MASKED = -1e30
UNSELECTED = 3e38
DILATIONS = (1, 4, 16)
HALF_WINDOW = 64
MAX_DILATION = 16
REL_BUCKETS = 32
REL_MAX_DISTANCE = 1024
HYENA_EMB_DIM = 33
PEER_KEYS = 128
PEER_HEADS = 8
PEER_TOPK = 16
PEER_EXPERT_TILE = 1024

LANES = 128
V7X_VMEM_LIMIT_BYTES = 58 * 1024 * 1024


def _cparams(*sem):
    return pltpu.CompilerParams(dimension_semantics=sem, vmem_limit_bytes=V7X_VMEM_LIMIT_BYTES)


def _dot(a, b, **kw):
    return jnp.dot(a, b, preferred_element_type=F32, **kw)


def _rms(x):
    return x * lax.rsqrt(jnp.mean(x * x, axis=-1, keepdims=True) + RMS_EPS)


def _modulated_norm(x, gain, scale, shift):
    return (_rms(x) * gain) * (1.0 + scale) + shift


def _ada_kernel(c_ref, w_ref, b_ref, o_ref):
    c = c_ref[...]
    a = c / (1.0 + jnp.exp(-c))
    o_ref[...] = _dot(a, w_ref[...], precision=HIGHEST) + b_ref[...]


def _ada(c_all, w_ada, b_ada):
    nb, d = c_all.shape
    n = w_ada.shape[1]
    tn = 768
    return pl.pallas_call(
        _ada_kernel,
        grid=(n // tn,),
        in_specs=[pl.BlockSpec((nb, d), lambda j: (0, 0)),
                  pl.BlockSpec((d, tn), lambda j: (0, j)),
                  pl.BlockSpec((1, tn), lambda j: (0, j))],
        out_specs=pl.BlockSpec((nb, tn), lambda j: (0, j)),
        out_shape=jax.ShapeDtypeStruct((nb, n), F32),
        compiler_params=_cparams("arbitrary"),
        name="ada",
    )(c_all, w_ada, b_ada.reshape(1, n))


def _head_rms(y, gain, ones_bd):
    outs = []
    for c in range(y.shape[1] // LANES):
        yc = y[:, c * LANES:(c + 1) * LANES]
        sq = yc * yc
        hi = sq.astype(BF16)
        lo = (sq - hi.astype(F32)).astype(BF16)
        ss = _dot(hi, ones_bd) + _dot(lo, ones_bd)
        outs.append(yc * lax.rsqrt(ss * (1.0 / HEAD_DIM) + RMS_EPS))
    return jnp.concatenate(outs, axis=1) * gain


def _qkv_kernel(x_ref, ada_ref, g_ref, w_ref, gq_ref, gk_ref, bd_ref, *outs):
    h = _modulated_norm(x_ref[0], g_ref[...], ada_ref[0, 1:2, :], ada_ref[0, 0:1, :]).astype(BF16)
    y = _dot(h, w_ref[...])
    w = y.shape[1] // 3
    bd = bd_ref[...]
    q = _head_rms(y[:, :w], gq_ref[...], bd) * (HEAD_DIM ** -0.5)
    k = _head_rms(y[:, w:2 * w], gk_ref[...], bd)
    v = y[:, 2 * w:]
    for j, t in enumerate((q, k, v)):
        tb = t.astype(BF16)
        outs[3 * j][0] = tb
        outs[3 * j + 1][0, 0] = tb
        outs[3 * j + 2][0, 0] = tb


def _qkv_proj(x, ada_g, norm_g, w_qkv, gq, gk, ones_bd):
    b, L, d = x.shape
    r = MAX_DILATION
    m = L // r
    w = w_qkv.shape[1] // 3
    xv = x.reshape(b, m, r * d)
    shapes, specs = [], []
    for _ in range(3):
        shapes += [jax.ShapeDtypeStruct((b, m, r * w), BF16),
                   jax.ShapeDtypeStruct((b, 4, m, 4 * w), BF16),
                   jax.ShapeDtypeStruct((b, r, m, w), BF16)]
        specs += [pl.BlockSpec((1, m, w), lambda bi, ri: (bi, 0, ri)),
                  pl.BlockSpec((1, 1, m, w), lambda bi, ri: (bi, ri % 4, 0, ri // 4)),
                  pl.BlockSpec((1, 1, m, w), lambda bi, ri: (bi, ri, 0, 0))]
    outs = pl.pallas_call(
        _qkv_kernel,
        grid=(b, r),
        in_specs=[pl.BlockSpec((1, m, d), lambda bi, ri: (bi, 0, ri)),
                  pl.BlockSpec((1, 6, d), lambda bi, ri: (bi, 0, 0)),
                  pl.BlockSpec((1, d), lambda bi, ri: (0, 0)),
                  pl.BlockSpec((d, 3 * w), lambda bi, ri: (0, 0)),
                  pl.BlockSpec((1, w), lambda bi, ri: (0, 0)),
                  pl.BlockSpec((1, w), lambda bi, ri: (0, 0)),
                  pl.BlockSpec((LANES, LANES), lambda bi, ri: (0, 0))],
        out_specs=specs,
        out_shape=shapes,
        compiler_params=_cparams("arbitrary", "arbitrary"),
        name="qkv_proj",
    )(xv, ada_g, norm_g, w_qkv, gq, gk, ones_bd)
    res = []
    for j in range(3):
        nat, p4, p16 = outs[3 * j:3 * j + 3]
        res.append((nat.reshape(b, L, w), p4.reshape(b * 4, L // 4, w), p16.reshape(b * r, m, w)))
    return res


def _hyproj_kernel(x_ref, ada_ref, g_ref, w_ref, o_ref):
    h = _modulated_norm(x_ref[0], g_ref[...], ada_ref[0, 1:2, :], ada_ref[0, 0:1, :]).astype(BF16)
    o_ref[0] = _dot(h, w_ref[...]).astype(BF16)


def _hy_proj(x, ada_g, norm_g, w_hy):
    b, L, d = x.shape
    n = w_hy.shape[1]
    tm = 512
    return pl.pallas_call(
        _hyproj_kernel,
        grid=(b, L // tm),
        in_specs=[pl.BlockSpec((1, tm, d), lambda bi, i: (bi, i, 0)),
                  pl.BlockSpec((1, 6, d), lambda bi, i: (bi, 0, 0)),
                  pl.BlockSpec((1, d), lambda bi, i: (0, 0)),
                  pl.BlockSpec((d, n), lambda bi, i: (0, 0))],
        out_specs=pl.BlockSpec((1, tm, n), lambda bi, i: (bi, i, 0)),
        out_shape=jax.ShapeDtypeStruct((b, L, n), BF16),
        compiler_params=_cparams("arbitrary", "arbitrary"),
        name="hy_proj",
    )(x, ada_g, norm_g, w_hy)


def _att_kernel(q_ref, kp_ref, kc_ref, kn_ref, vp_ref, vc_ref, vn_ref, bias_ref, o_ref, l_ref):
    width = q_ref.shape[2]
    first = lax.broadcasted_iota(jnp.int32, (1, LANES), 1) < HEAD_DIM
    for hp in range(width // LANES):
        sl = slice(hp * LANES, (hp + 1) * LANES)
        q = q_ref[0, :, sl]
        k = jnp.concatenate([kp_ref[0, :, sl], kc_ref[0, :, sl], kn_ref[0, :, sl]], axis=0)
        v = jnp.concatenate([vp_ref[0, :, sl], vc_ref[0, :, sl], vn_ref[0, :, sl]], axis=0)
        o_pair, l_pair = [], []
        for hh in range(2):
            sel = first if hh == 0 else jnp.logical_not(first)
            qh = jnp.where(sel, q, jnp.zeros_like(q))
            s = lax.dot_general(qh, k, (((1,), (1,)), ((), ())), preferred_element_type=F32)
            s = s + bias_ref[0, 2 * hp + hh]
            mx = jnp.max(s, axis=-1, keepdims=True)
            p = jnp.exp(s - mx)
            den = jnp.sum(p, axis=-1, keepdims=True)
            o_pair.append(_dot(p.astype(BF16), v) / den)
            l_pair.append(mx + jnp.log(den))
        o_ref[0, :, sl] = jnp.where(first, o_pair[0], o_pair[1])
        l_ref[0, :, sl] = jnp.where(first, l_pair[0], l_pair[1])


def _t5_bucket_of(rel):
    half = REL_BUCKETS // 2
    exact = half // 2
    n = jnp.abs(rel)
    nf = jnp.maximum(n, 1).astype(F32)
    large = exact + (jnp.log(nf / exact) / math.log(REL_MAX_DISTANCE / exact) * (half - exact)).astype(jnp.int32)
    large = jnp.minimum(large, half - 1)
    return jnp.where(rel > 0, half, 0) + jnp.where(n < exact, n, large)


def _band_bias(rel_bias, dilation):
    tq, tk, hw = 2 * HALF_WINDOW, 4 * HALF_WINDOW, HALF_WINDOW
    i = jnp.arange(tq)[:, None]
    j = jnp.arange(tk)[None, :]
    d = j - hw - i
    onehot = (_t5_bucket_of(d * dilation)[..., None] == jnp.arange(REL_BUCKETS)).astype(F32)
    bias = jnp.einsum("ijb,bh->hij", onehot, rel_bias.astype(F32), precision=HIGHEST)
    band = jnp.abs(d) <= hw
    out = []
    for var in range(4):
        ok = band
        if var & 1:
            ok = ok & (j >= hw)
        if var & 2:
            ok = ok & (j < tk - hw)
        out.append(jnp.where(ok[None], bias, MASKED))
    return jnp.stack(out)


def _band_attention(q, k, v, bias, batch, dilation):
    bp, m, w = q.shape
    tq, hw = 2 * HALF_WINDOW, HALF_WINDOW
    nq = m // tq
    nh = w // HEAD_DIM

    def var_idx(qi):
        return (qi == 0).astype(jnp.int32) + 2 * (qi == nq - 1).astype(jnp.int32)

    prev = lambda b, qi: (b, jnp.maximum(2 * qi - 1, 0), 0)
    cur = lambda b, qi: (b, qi, 0)
    nxt = lambda b, qi: (b, jnp.minimum(2 * qi + 2, 2 * nq - 1), 0)
    out_map = lambda b, qi: (b // dilation, qi, b % dilation)
    o, l = pl.pallas_call(
        _att_kernel,
        grid=(bp, nq),
        in_specs=[pl.BlockSpec((1, tq, w), cur),
                  pl.BlockSpec((1, hw, w), prev), pl.BlockSpec((1, tq, w), cur), pl.BlockSpec((1, hw, w), nxt),
                  pl.BlockSpec((1, hw, w), prev), pl.BlockSpec((1, tq, w), cur), pl.BlockSpec((1, hw, w), nxt),
                  pl.BlockSpec((1, nh, tq, 2 * tq), lambda b, qi: (var_idx(qi), 0, 0, 0))],
        out_specs=[pl.BlockSpec((1, tq, w), out_map), pl.BlockSpec((1, tq, w), out_map)],
        out_shape=[jax.ShapeDtypeStruct((batch, m, dilation * w), F32)] * 2,
        compiler_params=_cparams("arbitrary", "arbitrary"),
        name=f"band_att_d{dilation}",
    )(q, k, k, k, v, v, v, bias)
    L = m * dilation
    return o.reshape(batch, L, w), l.reshape(batch, L, w)


def _hy_pre_kernel(x0_ref, x1_ref, v_ref, w_ref, b_ref, z_ref, x0c_ref):
    L, tn = x0_ref.shape[1], x0_ref.shape[2]
    row = lax.broadcasted_iota(jnp.int32, (L, tn), 0)

    def conv(ref, s):
        u = ref[0].astype(F32)
        um = jnp.where(row == 0, 0.0, pltpu.roll(u, 1, 0))
        up = jnp.where(row == L - 1, 0.0, pltpu.roll(u, L - 1, 0))
        return um * w_ref[0, s:s + 1, :] + u * w_ref[1, s:s + 1, :] + up * w_ref[2, s:s + 1, :] + b_ref[s:s + 1, :]

    x0c_ref[0] = conv(x0_ref, 0).astype(BF16)
    z_ref[0] = (conv(v_ref, 2) * conv(x1_ref, 1)).astype(BF16)


def _hy_pre(hy, short_w, short_b):
    b, L, n3 = hy.shape
    c = n3 // 3
    tn = LANES
    nt = c // tn
    w3 = short_w.reshape(3, 3, c)
    b3 = short_b.reshape(3, c)
    stream = lambda s: pl.BlockSpec((1, L, tn), lambda bi, n: (bi, 0, s * nt + n))
    return pl.pallas_call(
        _hy_pre_kernel,
        grid=(b, nt),
        in_specs=[stream(0), stream(1), stream(2),
                  pl.BlockSpec((3, 3, tn), lambda bi, n: (0, 0, n)),
                  pl.BlockSpec((3, tn), lambda bi, n: (0, n))],
        out_specs=[pl.BlockSpec((1, L, tn), lambda bi, n: (bi, 0, n))] * 2,
        out_shape=[jax.ShapeDtypeStruct((b, L, c), BF16)] * 2,
        compiler_params=_cparams("arbitrary", "arbitrary"),
        name="hy_pre",
    )(hy, hy, hy, w3, b3)


def _filt_kernel(z_ref, w1_ref, b1_ref, w2_ref, b2_ref, w3_ref, b3_ref, sf_ref, dec_ref, o_ref):
    z = z_ref[...]
    tl = z.shape[0]
    c = dec_ref.shape[1]
    h = jnp.sin(sf_ref[0:1, :] * (_dot(z, w1_ref[...], precision=HIGHEST) + b1_ref[...]))
    h = jnp.sin(sf_ref[1:2, :] * (_dot(h, w2_ref[...], precision=HIGHEST) + b2_ref[...]))
    hf = _dot(h, w3_ref[...], precision=HIGHEST) + b3_ref[...]
    t = z[:, 0:1]
    dec = jnp.abs(dec_ref[...])
    fwd = hf[:, :c] * jnp.exp(-t * dec[0:1, :])
    bwd = hf[:, c:] * jnp.exp(-t * dec[1:2, :])
    row = pl.program_id(0) * tl + lax.broadcasted_iota(jnp.int32, (tl, c), 0)
    o_ref[0] = fwd.astype(BF16)
    o_ref[1] = jnp.where(row == 0, 0.0, bwd).astype(BF16)


def _position_embedding(L):
    t = jnp.linspace(0.0, 1.0, L, dtype=F32)[:, None]
    bands = (HYENA_EMB_DIM - 1) // 2
    freqs = jnp.linspace(1e-4, bands - 1, bands, dtype=F32)[None, :]
    wpos = 2.0 * math.pi * jnp.arange(L, dtype=F32)[:, None] / L
    z = jnp.concatenate([t, jnp.cos(freqs * wpos), -jnp.sin(freqs * wpos)], axis=-1)
    return jnp.pad(z, ((0, 0), (0, LANES - HYENA_EMB_DIM)))


def _hyena_filter_taps(L, w1, b1, w2, b2, w3, b3, sin_freq, decay):
    fw = w1.shape[1]
    c = decay.shape[1]
    tl = 512
    zemb = _position_embedding(L)
    w1p = jnp.pad(w1, ((0, LANES - HYENA_EMB_DIM), (0, 0)))
    full = lambda shape: pl.BlockSpec(shape, lambda i: (0,) * len(shape))
    return pl.pallas_call(
        _filt_kernel,
        grid=(L // tl,),
        in_specs=[pl.BlockSpec((tl, LANES), lambda i: (i, 0)),
                  full((LANES, fw)), full((1, fw)), full((fw, fw)), full((1, fw)),
                  full((fw, 2 * c)), full((1, 2 * c)), full((2, fw)), full((2, c))],
        out_specs=pl.BlockSpec((2, tl, c), lambda i: (0, i, 0)),
        out_shape=jax.ShapeDtypeStruct((2, L, c), BF16),
        compiler_params=_cparams("arbitrary"),
        name="hy_filter_taps",
    )(zemb, w1p, b1.reshape(1, fw), w2, b2.reshape(1, fw), w3, b3.reshape(1, 2 * c), sin_freq, decay)


def _dft_matrix_kernel(coarse_ref, fine_ref, o_ref):
    ca, sa = coarse_ref[0, 0], coarse_ref[0, 1]
    cb, sb = fine_ref[0], fine_ref[1]
    o_ref[0] = (ca * cb - sa * sb).astype(BF16)
    o_ref[1] = (sa * cb + ca * sb).astype(BF16)


def _shifted_dft_matrix(L):
    nk1 = L // LANES
    odd = 2 * jnp.arange(L, dtype=jnp.int32) + 1
    unit = 2.0 * math.pi / (8 * L)
    ang_a = ((2 * LANES * jnp.arange(nk1, dtype=jnp.int32)[:, None] * odd[None, :]) % (8 * L)).astype(F32) * unit
    ang_b = ((odd[:LANES, None] * odd[None, :]) % (8 * L)).astype(F32) * unit
    coarse = jnp.stack([jnp.cos(ang_a), jnp.sin(ang_a)], axis=1).reshape(nk1, 2, 1, L)
    fine = jnp.stack([jnp.cos(ang_b), jnp.sin(ang_b)])
    return pl.pallas_call(
        _dft_matrix_kernel,
        grid=(nk1,),
        in_specs=[pl.BlockSpec((1, 2, 1, L), lambda i: (i, 0, 0, 0)),
                  pl.BlockSpec((2, LANES, L), lambda i: (0, 0, 0))],
        out_specs=pl.BlockSpec((2, LANES, L), lambda i: (0, i, 0)),
        out_shape=jax.ShapeDtypeStruct((2, L, L), BF16),
        compiler_params=_cparams("arbitrary"),
        name="dft_matrix",
    )(coarse, fine)


def _filt_dft_kernel(m_ref, r_ref, ct_ref, st_ref, h_ref):
    a = _dot(m_ref[0], r_ref[0])
    b = _dot(m_ref[1], r_ref[0])
    c = _dot(m_ref[0], r_ref[1])
    d = _dot(m_ref[1], r_ref[1])
    ct, st = ct_ref[...], st_ref[...]
    h_ref[0] = ct * (a + c) + st * (b + d)
    h_ref[1] = st * (a - c) + ct * (d - b)


def _filter_spectrum(dft, taps):
    _, L, c = taps.shape
    tk, tn = 256, min(256, c)
    theta = (2 * jnp.arange(L, dtype=F32) + 1) * (math.pi / (4 * L))
    ct = jnp.cos(theta)[:, None]
    st = jnp.sin(theta)[:, None]
    return pl.pallas_call(
        _filt_dft_kernel,
        grid=(c // tn, L // tk),
        in_specs=[pl.BlockSpec((2, tk, L), lambda n, k: (0, k, 0)),
                  pl.BlockSpec((2, L, tn), lambda n, k: (0, 0, n)),
                  pl.BlockSpec((tk, 1), lambda n, k: (k, 0)),
                  pl.BlockSpec((tk, 1), lambda n, k: (k, 0))],
        out_specs=pl.BlockSpec((2, tk, tn), lambda n, k: (0, k, n)),
        out_shape=jax.ShapeDtypeStruct((2, L, c), F32),
        compiler_params=_cparams("arbitrary", "arbitrary"),
        name="hy_filter_dft",
    )(dft, taps, ct, st)


def _hy_fwd_kernel(m_ref, z_ref, h_ref, y_ref):
    zc = _dot(m_ref[0], z_ref[0])
    zs = _dot(m_ref[1], z_ref[0])
    hr, hi = h_ref[0], h_ref[1]
    y_ref[0, 0] = (zc * hr + zs * hi).astype(BF16)
    y_ref[0, 1] = (zc * hi - zs * hr).astype(BF16)


def _hy_forward(dft, z, spec):
    b, L, c = z.shape
    tk, tn = 256, min(512, c)
    return pl.pallas_call(
        _hy_fwd_kernel,
        grid=(b, c // tn, L // tk),
        in_specs=[pl.BlockSpec((2, tk, L), lambda bi, n, k: (0, k, 0)),
                  pl.BlockSpec((1, L, tn), lambda bi, n, k: (bi, 0, n)),
                  pl.BlockSpec((2, tk, tn), lambda bi, n, k: (0, k, n))],
        out_specs=pl.BlockSpec((1, 2, tk, tn), lambda bi, n, k: (bi, 0, k, n)),
        out_shape=jax.ShapeDtypeStruct((b, 2, L, c), BF16),
        compiler_params=_cparams("arbitrary", "arbitrary", "arbitrary"),
        name="hy_dft_fwd",
    )(dft, z, spec)


def _hy_inv_kernel(m_ref, y_ref, z_ref, x0_ref, bd_ref, o_ref, *, scale):
    conv = _dot(m_ref[0], y_ref[0, 0]) - _dot(m_ref[1], y_ref[0, 1])
    y = conv * scale + z_ref[0].astype(F32) * bd_ref[...]
    o_ref[0] = x0_ref[0].astype(F32) * y


def _hy_inverse(dft, yspec, z, x0c, bias_d):
    b, L, c = z.shape
    tt, tn = 256, min(512, c)
    return pl.pallas_call(
        functools.partial(_hy_inv_kernel, scale=1.0 / L),
        grid=(b, c // tn, L // tt),
        in_specs=[pl.BlockSpec((2, tt, L), lambda bi, n, t: (0, t, 0)),
                  pl.BlockSpec((1, 2, L, tn), lambda bi, n, t: (bi, 0, 0, n)),
                  pl.BlockSpec((1, tt, tn), lambda bi, n, t: (bi, t, n)),
                  pl.BlockSpec((1, tt, tn), lambda bi, n, t: (bi, t, n)),
                  pl.BlockSpec((1, tn), lambda bi, n, t: (0, n))],
        out_specs=pl.BlockSpec((1, tt, tn), lambda bi, n, t: (bi, t, n)),
        out_shape=jax.ShapeDtypeStruct((b, L, c), F32),
        compiler_params=_cparams("arbitrary", "arbitrary", "arbitrary"),
        name="hy_dft_inv",
    )(dft, yspec, z, x0c, bias_d.reshape(1, c))


def _out_kernel(o1_ref, o2_ref, o3_ref, l1_ref, l2_ref, l3_ref, hy_ref, x_ref, ada_ref,
                ga_ref, gh_ref, w_ref, g2_ref, x1_ref, h2_ref):
    la, lb, lc = l1_ref[0], l2_ref[0], l3_ref[0]
    lm = jnp.maximum(jnp.maximum(la, lb), lc)
    wa, wb, wc = jnp.exp(la - lm), jnp.exp(lb - lm), jnp.exp(lc - lm)
    att = (wa * o1_ref[0] + wb * o2_ref[0] + wc * o3_ref[0]) / (wa + wb + wc)
    wa_ = att.shape[1]
    an = (_rms(att) * ga_ref[...]).astype(BF16)
    hn = (_rms(hy_ref[0]) * gh_ref[...]).astype(BF16)
    res = _dot(an, w_ref[:wa_, :]) + _dot(hn, w_ref[wa_:, :])
    x1 = x_ref[0] + ada_ref[0, 2:3, :] * res
    x1_ref[0] = x1
    h2_ref[0] = _modulated_norm(x1, g2_ref[...], ada_ref[0, 4:5, :], ada_ref[0, 3:4, :]).astype(BF16)


def _out_proj(att_parts, hyo, x, ada_g, att_g, hy_g, w_out, norm2_g):
    b, L, d = x.shape
    wa = hyo.shape[2]
    tm = 256
    tok = lambda width: pl.BlockSpec((1, tm, width), lambda bi, i: (bi, i, 0))
    full = lambda shape: pl.BlockSpec(shape, lambda bi, i: (0,) * len(shape))
    (o1, l1), (o2, l2), (o3, l3) = att_parts
    return pl.pallas_call(
        _out_kernel,
        grid=(b, L // tm),
        in_specs=[tok(wa)] * 7 + [tok(d), pl.BlockSpec((1, 6, d), lambda bi, i: (bi, 0, 0)),
                                  full((1, wa)), full((1, wa)), full((d, d)), full((1, d))],
        out_specs=[tok(d), tok(d)],
        out_shape=[jax.ShapeDtypeStruct((b, L, d), F32), jax.ShapeDtypeStruct((b, L, d), BF16)],
        compiler_params=_cparams("arbitrary", "arbitrary"),
        name="out_proj",
    )(o1, o2, o3, l1, l2, l3, hyo, x, ada_g, att_g, hy_g, w_out, norm2_g)


def _peer_q_kernel(h_ref, w_ref, sk_ref, s_ref):
    q = _dot(h_ref[...], w_ref[...]).astype(BF16)
    for c in range(sk_ref.shape[0]):
        qc = q[:, c * PEER_KEYS:(c + 1) * PEER_KEYS]
        s_ref[c] = lax.dot_general(sk_ref[c], qc, (((1,), (1,)), ((), ())), preferred_element_type=F32)


def _peer_scores(h2, w_query, sub_keys):
    t, d = h2.shape
    nc, nk, qd = sub_keys.shape
    tm = 512
    return pl.pallas_call(
        _peer_q_kernel,
        grid=(t // tm,),
        in_specs=[pl.BlockSpec((tm, d), lambda i: (i, 0)),
                  pl.BlockSpec((d, nc * qd), lambda i: (0, 0)),
                  pl.BlockSpec((nc, nk, qd), lambda i: (0, 0, 0))],
        out_specs=pl.BlockSpec((nc, nk, tm), lambda i: (0, 0, i)),
        out_shape=jax.ShapeDtypeStruct((nc, nk, t), F32),
        compiler_params=_cparams("arbitrary"),
        name="peer_scores",
    )(h2, w_query, sub_keys)


def _gelu_tanh(x):
    return x * (0.5 * (1.0 + jnp.tanh(math.sqrt(2.0 / math.pi) * (x + 0.044715 * (x * x * x)))))


def _peer_main_kernel(ht_ref, u_ref, vt_ref, th_ref, s2_ref, e1_ref, e2_ref, o_ref, at_ref, pt_ref):
    e = pl.program_id(1)
    te = u_ref.shape[0]
    ni = te // PEER_KEYS

    @pl.when(e == 0)
    def _():
        o_ref[...] = jnp.zeros_like(o_ref)

    at_ref[...] = _dot(u_ref[...], ht_ref[...])
    for ii in range(ni):
        idx = e * ni + ii
        rows = slice(ii * PEER_KEYS, (ii + 1) * PEER_KEYS)
        for c in range(ht_ref.shape[1] // LANES):
            cols = slice(c * LANES, (c + 1) * LANES)
            w = jnp.zeros((PEER_KEYS, LANES), F32)
            for h in range(PEER_HEADS):
                chosen = s2_ref[h, c] >= th_ref[h, c, pl.ds(idx, 1), :]
                gate = e1_ref[h, c, pl.ds(idx, 1), :] * e2_ref[h, c]
                w = w + jnp.where(chosen, gate, 0.0)
            pt_ref[rows, cols] = (_gelu_tanh(at_ref[rows, cols]) * w).astype(BF16)
    o_ref[...] += _dot(vt_ref[0], pt_ref[...])


def _peer_experts(h2t, u_tab, vt_tab, theta, s2m, e1, e2):
    d, t = h2t.shape
    ne = u_tab.shape[0]
    tm, te = 512, PEER_EXPERT_TILE
    nt = ne // te
    gate =pl.BlockSpec((PEER_HEADS, tm // LANES, PEER_KEYS, LANES), lambda i, e: (0, i, 0, 0))
    return pl.pallas_call(
        _peer_main_kernel,
        grid=(t // tm, nt),
        in_specs=[pl.BlockSpec((d, tm), lambda i, e: (0, i)),
                  pl.BlockSpec((te, d), lambda i, e: (e, 0)),
                  pl.BlockSpec((1, d, te), lambda i, e: (e, 0, 0)),
                  gate, gate, gate, gate],
        out_specs=pl.BlockSpec((d, tm), lambda i, e: (0, i)),
        out_shape=jax.ShapeDtypeStruct((d, t), F32),
        scratch_shapes=[pltpu.VMEM((te, tm), F32), pltpu.VMEM((te, tm), BF16)],
        compiler_params=_cparams("arbitrary", "arbitrary"),
        name="peer_experts",
    )(h2t, u_tab, vt_tab, theta, s2m, e1, e2)


def _fin_kernel(ot_ref, x1_ref, ada_ref, y_ref):
    y_ref[...] = x1_ref[...] + ada_ref[0, 5:6, :] * ot_ref[...].T


def _final_residual(out_t, x1, ada_g, seq_len):
    t, d = x1.shape
    tm = 256
    per_seq = seq_len // tm
    return pl.pallas_call(
        _fin_kernel,
        grid=(t // tm,),
        in_specs=[pl.BlockSpec((d, tm), lambda i: (0, i)),
                  pl.BlockSpec((tm, d), lambda i: (i, 0)),
                  pl.BlockSpec((1, 6, d), lambda i: (i // per_seq, 0, 0))],
        out_specs=pl.BlockSpec((tm, d), lambda i: (i, 0)),
        out_shape=jax.ShapeDtypeStruct((t, d), F32),
        compiler_params=_cparams("arbitrary"),
        name="final_residual",
    )(out_t, x1, ada_g)


def _extract_top(work_ref, vals_ref, count, idx_ref=None):
    n, tm = work_ref.shape
    rows = lax.broadcasted_iota(jnp.int32, (n, tm), 0)

    def body(r, carry):
        s = work_ref[...]
        m = jnp.max(s, axis=0, keepdims=True)
        idx = jnp.min(jnp.where(s == m, rows, n), axis=0, keepdims=True)
        work_ref[...] = jnp.where(rows == idx, MASKED, s)
        vals_ref[pl.ds(r, 1), :] = m
        if idx_ref is not None:
            idx_ref[pl.ds(r, 1), :] = idx
        return carry

    lax.fori_loop(0, count, body, 0)


_CAND_SECOND = 8
_CAND_ROWS = PEER_TOPK + 7 * _CAND_SECOND + (PEER_TOPK - 8)


def _peer_gate_kernel(s_ref, th_ref, s2m_ref, e1_ref, e2_ref, work_ref, v1_ref, v2_ref,
                      cand_ref, best_ref, idx_ref):
    k = PEER_TOPK
    s1 = s_ref[0]
    s2 = s_ref[1]
    work_ref[...] = s1
    _extract_top(work_ref, v1_ref, k, idx_ref)
    in1 = work_ref[...] == MASKED
    work_ref[...] = s2
    _extract_top(work_ref, v2_ref, k)
    in2 = work_ref[...] == MASKED

    v1 = v1_ref[...]
    v2 = v2_ref[...]
    cand_ref[0:k, :] = v1[0:1] + v2
    brow = lax.broadcasted_iota(jnp.int32, (_CAND_SECOND, v2.shape[1]), 0)
    for a in range(1, 8):
        blk = v1[a:a + 1] + v2[0:_CAND_SECOND]
        lo = k + (a - 1) * _CAND_SECOND
        cand_ref[lo:lo + _CAND_SECOND, :] = jnp.where(brow < k // (a + 1), blk, MASKED)
    cand_ref[k + 7 * _CAND_SECOND:, :] = v1[8:k] + v2[0:1]
    _extract_top(cand_ref, best_ref, k)

    best = best_ref[...]
    zsum = jnp.sum(jnp.exp(best - best[0:1]), axis=0, keepdims=True)
    tau = best[k - 1:k]

    rows = lax.broadcasted_iota(jnp.int32, s1.shape, 0)
    theta = jnp.full(s1.shape, UNSELECTED, F32)
    for a in range(k):
        th_a = jnp.min(jnp.where(v1[a:a + 1] + v2 >= tau, v2, UNSELECTED), axis=0, keepdims=True)
        theta = jnp.where(rows == idx_ref[a:a + 1, :], th_a, theta)

    s2m = jnp.where(in2, s2, MASKED)
    e1 = jnp.where(in1, jnp.exp(s1 - v1[0:1]) / zsum, 0.0)
    e2 = jnp.where(in2, jnp.exp(s2 - v2[0:1]), 0.0)
    for c in range(s1.shape[1] // LANES):
        cols = slice(c * LANES, (c + 1) * LANES)
        th_ref[0, c] = theta[:, cols]
        s2m_ref[0, c] = s2m[:, cols]
        e1_ref[0, c] = e1[:, cols]
        e2_ref[0, c] = e2[:, cols]


def _peer_gates(scores):
    nc, nk, t = scores.shape
    nh = nc // 2
    tm = 512
    ncl = tm // LANES
    big = pl.BlockSpec((1, ncl, nk, LANES), lambda i, h: (h, i, 0, 0))
    big_shape = jax.ShapeDtypeStruct((nh, t // LANES, nk, LANES), F32)
    return pl.pallas_call(
        _peer_gate_kernel,
        grid=(t // tm, nh),
        in_specs=[pl.BlockSpec((2, nk, tm), lambda i, h: (h, 0, i))],
        out_specs=[big, big, big, big],
        out_shape=[big_shape] * 4,
        scratch_shapes=[pltpu.VMEM((nk, tm), F32), pltpu.VMEM((PEER_TOPK, tm), F32),
                        pltpu.VMEM((PEER_TOPK, tm), F32), pltpu.VMEM((_CAND_ROWS, tm), F32),
                        pltpu.VMEM((PEER_TOPK, tm), F32), pltpu.VMEM((PEER_TOPK, tm), jnp.int32)],
        compiler_params=_cparams("arbitrary", "arbitrary"),
        name="peer_gates",
    )(scores)


def _encode_group(x, ada_g, prm):
    b, L, d = x.shape
    (q1, q4, q16), (k1, k4, k16), (v1, v4, v16) = _qkv_proj(
        x, ada_g, prm["norm1_g"], prm["w_qkv"], prm["gq"], prm["gk"], prm["ones_bd"])
    att_parts = [
        _band_attention(q1, k1, v1, prm["bias"][0], b, 1),
        _band_attention(q4, k4, v4, prm["bias"][1], b, 4),
        _band_attention(q16, k16, v16, prm["bias"][2], b, 16),
    ]
    hy = _hy_proj(x, ada_g, prm["norm1_g"], prm["w_hy"])
    z, x0c = _hy_pre(hy, prm["short_w"], prm["short_b"])
    dft = _shifted_dft_matrix(L)
    taps = _hyena_filter_taps(L, *prm["filter"])
    spec = _filter_spectrum(dft, taps)
    yspec = _hy_forward(dft, z, spec)
    hyo = _hy_inverse(dft, yspec, z, x0c, prm["bias_d"])
    x1, h2 = _out_proj(att_parts, hyo, x, ada_g, prm["att_out_g"], prm["hyena_out_g"], prm["w_out"], prm["norm2_g"])
    t = b * L
    h2 = h2.reshape(t, d)
    scores = _peer_scores(h2, prm["w_query"], prm["sub_keys"])
    theta, s2m, e1, e2 = _peer_gates(scores)
    out_t = _peer_experts(h2.T, prm["u_tab"], prm["vt_tab"], theta, s2m, e1, e2)
    y = _final_residual(out_t, x1.reshape(t, d), ada_g, L)
    return y.reshape(b, L, d)


def kernel(x_prompt, x_sample, c_prompt, c_sample, rel_bias, w_ada, b_ada, norm1_g, w_in, q_norm_g, k_norm_g, hyena_short_w, hyena_short_b, hyena_ffn_w1, hyena_ffn_b1, hyena_ffn_w2, hyena_ffn_b2, hyena_ffn_w3, hyena_ffn_b3, hyena_sin_freq, hyena_decay, hyena_bias_d, att_out_g, hyena_out_g, w_out, norm2_g, peer_w_query, peer_sub_keys, peer_u, peer_v):
    assert w_ada.shape[0] == 1, "single-layer problem"
    d = x_prompt.shape[-1]
    bp, bs = x_prompt.shape[0], x_sample.shape[0]
    att_w = att_out_g.shape[1]
    nheads = att_w // HEAD_DIM

    c_all = jnp.concatenate([c_prompt, c_sample], axis=0)
    pad = (-c_all.shape[0]) % 8
    ada = _ada(jnp.pad(c_all, ((0, pad), (0, 0))), w_ada[0], b_ada[0])
    ada = ada[:bp + bs].reshape(bp + bs, 6, d)

    w_in0 = w_in[0]
    lane = jnp.arange(LANES)
    prm = {
        "norm1_g": norm1_g[0].reshape(1, d),
        "w_qkv": w_in0[:, :3 * att_w].astype(BF16),
        "w_hy": w_in0[:, 3 * att_w:].astype(BF16),
        "gq": jnp.tile(q_norm_g[0], nheads).reshape(1, att_w),
        "gk": jnp.tile(k_norm_g[0], nheads).reshape(1, att_w),
        "ones_bd": (lane[:, None] // HEAD_DIM == lane[None, :] // HEAD_DIM).astype(BF16),
        "bias": [_band_bias(rel_bias, dil) for dil in DILATIONS],
        "short_w": hyena_short_w[0],
        "short_b": hyena_short_b[0],
        "filter": (hyena_ffn_w1[0], hyena_ffn_b1[0], hyena_ffn_w2[0], hyena_ffn_b2[0],
                   hyena_ffn_w3[0], hyena_ffn_b3[0], hyena_sin_freq[0], hyena_decay[0]),
        "bias_d": hyena_bias_d[0],
        "att_out_g": att_out_g[0].reshape(1, -1),
        "hyena_out_g": hyena_out_g[0].reshape(1, -1),
        "w_out": w_out[0].astype(BF16),
        "norm2_g": norm2_g[0].reshape(1, d),
        "w_query": peer_w_query[0].astype(BF16),
        "sub_keys": peer_sub_keys[0].reshape(2 * PEER_HEADS, PEER_KEYS, -1).astype(BF16),
        "u_tab": peer_u[0].astype(BF16),
        "vt_tab": peer_v[0].astype(BF16).reshape(-1, PEER_EXPERT_TILE, d).transpose(0, 2, 1),
    }
    y_prompt = _encode_group(x_prompt, ada[:bp], prm)
    y_sample = _encode_group(x_sample, ada[bp:], prm)
    return (y_prompt, y_sample)
```

```python
import functools
import math

import jax
import jax.numpy as jnp
from jax import lax
from jax.experimental import pallas as pl
from jax.experimental.pallas import tpu as pltpu

F32 = jnp.float32
BF16 = jnp.bfloat16
HIGHEST = lax.Precision.HIGHEST

HEAD_DIM = 64
RMS_EPS = 1e-6
MASKED = -1e30
UNSELECTED = 3e38
DILATIONS = (1, 4, 16)
HALF_WINDOW = 64
MAX_DILATION = 16
REL_BUCKETS = 32
REL_MAX_DISTANCE = 1024
HYENA_EMB_DIM = 33
PEER_KEYS = 128
PEER_HEADS = 8
PEER_TOPK = 16
PEER_EXPERT_TILE = 1024

LANES = 128
V7X_VMEM_LIMIT_BYTES = 58 * 1024 * 1024


def _cparams(*sem):
    return pltpu.CompilerParams(dimension_semantics=sem, vmem_limit_bytes=V7X_VMEM_LIMIT_BYTES)


def _dot(a, b, **kw):
    return jnp.dot(a, b, preferred_element_type=F32, **kw)


def _rms(x):
    return x * lax.rsqrt(jnp.mean(x * x, axis=-1, keepdims=True) + RMS_EPS)


def _modulated_norm(x, gain, scale, shift):
    return (_rms(x) * gain) * (1.0 + scale) + shift


def _ada_kernel(c_ref, w_ref, b_ref, o_ref):
    c = c_ref[...]
    a = c / (1.0 + jnp.exp(-c))
    o_ref[...] = _dot(a, w_ref[...], precision=HIGHEST) + b_ref[...]


def _ada(c_all, w_ada, b_ada):
    nb, d = c_all.shape
    n = w_ada.shape[1]
    tn = 768
    return pl.pallas_call(
        _ada_kernel,
        grid=(n // tn,),
        in_specs=[pl.BlockSpec((nb, d), lambda j: (0, 0)),
                  pl.BlockSpec((d, tn), lambda j: (0, j)),
                  pl.BlockSpec((1, tn), lambda j: (0, j))],
        out_specs=pl.BlockSpec((nb, tn), lambda j: (0, j)),
        out_shape=jax.ShapeDtypeStruct((nb, n), F32),
        compiler_params=_cparams("arbitrary"),
        name="ada",
    )(c_all, w_ada, b_ada.reshape(1, n))


def _head_rms(y, gain, ones_bd):
    outs = []
    for c in range(y.shape[1] // LANES):
        yc = y[:, c * LANES:(c + 1) * LANES]
        sq = yc * yc
        hi = sq.astype(BF16)
        lo = (sq - hi.astype(F32)).astype(BF16)
        ss = _dot(hi, ones_bd) + _dot(lo, ones_bd)
        outs.append(yc * lax.rsqrt(ss * (1.0 / HEAD_DIM) + RMS_EPS))
    return jnp.concatenate(outs, axis=1) * gain


def _qkv_kernel(x_ref, ada_ref, g_ref, w_ref, gq_ref, gk_ref, bd_ref, *refs):
    outs, stage_ref = refs[:-1], refs[-1]
    tm = x_ref.shape[1]
    h = _modulated_norm(x_ref[0], g_ref[...], ada_ref[0, 1:2, :], ada_ref[0, 0:1, :]).astype(BF16)
    y = _dot(h, w_ref[...])
    w = y.shape[1] // 3
    bd = bd_ref[...]
    q = _head_rms(y[:, :w], gq_ref[...], bd) * (HEAD_DIM ** -0.5)
    k = _head_rms(y[:, w:2 * w], gk_ref[...], bd)
    v = y[:, 2 * w:]
    for j, t in enumerate((q, k, v)):
        outs[3 * j][0] = t.astype(BF16)
        for c in range(w // LANES):
            cols = slice(c * LANES, (c + 1) * LANES)
            stage_ref[c] = t[:, cols]
            for dil, out in ((4, outs[3 * j + 1]), (MAX_DILATION, outs[3 * j + 2])):
                for r in range(dil):
                    out[0, r, :, cols] = stage_ref[c, pl.ds(r, tm // dil, stride=dil), :].astype(BF16)


def _qkv_proj(x, ada_g, norm_g, w_qkv, gq, gk, ones_bd):
    b, L, d = x.shape
    r = MAX_DILATION
    w = w_qkv.shape[1] // 3
    tm = 512
    shapes, specs = [], []
    for _ in range(3):
        shapes += [jax.ShapeDtypeStruct((b, L, w), BF16),
                   jax.ShapeDtypeStruct((b, 4, L // 4, w), BF16),
                   jax.ShapeDtypeStruct((b, r, L // r, w), BF16)]
        specs += [pl.BlockSpec((1, tm, w), lambda bi, i: (bi, i, 0)),
                  pl.BlockSpec((1, 4, tm // 4, w), lambda bi, i: (bi, 0, i, 0)),
                  pl.BlockSpec((1, r, tm // r, w), lambda bi, i: (bi, 0, i, 0))]
    outs = pl.pallas_call(
        _qkv_kernel,
        grid=(b, L // tm),
        in_specs=[pl.BlockSpec((1, tm, d), lambda bi, i: (bi, i, 0)),
                  pl.BlockSpec((1, 6, d), lambda bi, i: (bi, 0, 0)),
                  pl.BlockSpec((1, d), lambda bi, i: (0, 0)),
                  pl.BlockSpec((d, 3 * w), lambda bi, i: (0, 0)),
                  pl.BlockSpec((1, w), lambda bi, i: (0, 0)),
                  pl.BlockSpec((1, w), lambda bi, i: (0, 0)),
                  pl.BlockSpec((LANES, LANES), lambda bi, i: (0, 0))],
        out_specs=specs,
        out_shape=shapes,
        scratch_shapes=[pltpu.VMEM((w // LANES, tm, LANES), F32)],
        compiler_params=_cparams("arbitrary", "arbitrary"),
        name="qkv_proj",
    )(x, ada_g, norm_g, w_qkv, gq, gk, ones_bd)
    return [tuple(outs[3 * j:3 * j + 3]) for j in range(3)]


def _hyproj_kernel(x_ref, ada_ref, g_ref, w_ref, o_ref):
    h = _modulated_norm(x_ref[0], g_ref[...], ada_ref[0, 1:2, :], ada_ref[0, 0:1, :]).astype(BF16)
    o_ref[0] = _dot(h, w_ref[...]).astype(BF16)


def _hy_proj(x, ada_g, norm_g, w_hy):
    b, L, d = x.shape
    n = w_hy.shape[1]
    tm = 512
    return pl.pallas_call(
        _hyproj_kernel,
        grid=(b, L // tm),
        in_specs=[pl.BlockSpec((1, tm, d), lambda bi, i: (bi, i, 0)),
                  pl.BlockSpec((1, 6, d), lambda bi, i: (bi, 0, 0)),
                  pl.BlockSpec((1, d), lambda bi, i: (0, 0)),
                  pl.BlockSpec((d, n), lambda bi, i: (0, 0))],
        out_specs=pl.BlockSpec((1, tm, n), lambda bi, i: (bi, i, 0)),
        out_shape=jax.ShapeDtypeStruct((b, L, n), BF16),
        compiler_params=_cparams("arbitrary", "arbitrary"),
        name="hy_proj",
    )(x, ada_g, norm_g, w_hy)


def _att_kernel(q_ref, kp_ref, kc_ref, kn_ref, vp_ref, vc_ref, vn_ref, bias_ref, o_ref, l_ref, stage_ref):
    _, dil, tq, width = q_ref.shape
    first = lax.broadcasted_iota(jnp.int32, (1, LANES), 1) < HEAD_DIM
    for hp in range(width // LANES):
        sl = slice(hp * LANES, (hp + 1) * LANES)
        for r in range(dil):
            rows = pl.ds(r, tq, stride=dil) if dil > 1 else slice(None)
            q = q_ref[0, r, :, sl]
            k = jnp.concatenate([kp_ref[0, r, :, sl], kc_ref[0, r, :, sl], kn_ref[0, r, :, sl]], axis=0)
            v = jnp.concatenate([vp_ref[0, r, :, sl], vc_ref[0, r, :, sl], vn_ref[0, r, :, sl]], axis=0)
            o_pair, l_pair = [], []
            for hh in range(2):
                sel = first if hh == 0 else jnp.logical_not(first)
                qh = jnp.where(sel, q, jnp.zeros_like(q))
                s = lax.dot_general(qh, k, (((1,), (1,)), ((), ())), preferred_element_type=F32)
                s = s + bias_ref[0, 2 * hp + hh]
                mx = jnp.max(s, axis=-1, keepdims=True)
                p = jnp.exp(s - mx)
                den = jnp.sum(p, axis=-1, keepdims=True)
                o_pair.append(_dot(p.astype(BF16), v) / den)
                l_pair.append(mx + jnp.log(den))
            stage_ref[0, rows, :] = jnp.where(first, o_pair[0], o_pair[1])
            stage_ref[1, rows, :] = jnp.where(first, l_pair[0], l_pair[1])
        o_ref[0, :, sl] = stage_ref[0]
        l_ref[0, :, sl] = stage_ref[1]


def _t5_bucket_of(rel):
    half = REL_BUCKETS // 2
    exact = half // 2
    n = jnp.abs(rel)
    nf = jnp.maximum(n, 1).astype(F32)
    large = exact + (jnp.log(nf / exact) / math.log(REL_MAX_DISTANCE / exact) * (half - exact)).astype(jnp.int32)
    large = jnp.minimum(large, half - 1)
    return jnp.where(rel > 0, half, 0) + jnp.where(n < exact, n, large)


def _band_bias(rel_bias, dilation):
    tq, tk, hw = 2 * HALF_WINDOW, 4 * HALF_WINDOW, HALF_WINDOW
    i = jnp.arange(tq)[:, None]
    j = jnp.arange(tk)[None, :]
    d = j - hw - i
    onehot = (_t5_bucket_of(d * dilation)[..., None] == jnp.arange(REL_BUCKETS)).astype(F32)
    bias = jnp.einsum("ijb,bh->hij", onehot, rel_bias.astype(F32), precision=HIGHEST)
    band = jnp.abs(d) <= hw
    out = []
    for var in range(4):
        ok = band
        if var & 1:
            ok = ok & (j >= hw)
        if var & 2:
            ok = ok & (j < tk - hw)
        out.append(jnp.where(ok[None], bias, MASKED))
    return jnp.stack(out)


def _band_attention(q, k, v, bias):
    batch, dilation, m, w = q.shape
    tq, hw = 2 * HALF_WINDOW, HALF_WINDOW
    nq = m // tq
    wl = max(LANES, w // max(1, dilation // 2))
    nhl = wl // HEAD_DIM

    def var_idx(qi):
        return (qi == 0).astype(jnp.int32) + 2 * (qi == nq - 1).astype(jnp.int32)

    prev = lambda b, qi, lb: (b, 0, jnp.maximum(2 * qi - 1, 0), lb)
    cur = lambda b, qi, lb: (b, 0, qi, lb)
    nxt = lambda b, qi, lb: (b, 0, jnp.minimum(2 * qi + 2, 2 * nq - 1), lb)
    out_spec = pl.BlockSpec((1, tq * dilation, wl), lambda b, qi, lb: (b, qi, lb))
    side = pl.BlockSpec((1, dilation, hw, wl), prev), pl.BlockSpec((1, dilation, hw, wl), nxt)
    mid = pl.BlockSpec((1, dilation, tq, wl), cur)
    return pl.pallas_call(
        _att_kernel,
        grid=(batch, nq, w // wl),
        in_specs=[mid, side[0], mid, side[1], side[0], mid, side[1],
                  pl.BlockSpec((1, nhl, tq, 2 * tq), lambda b, qi, lb: (var_idx(qi), lb, 0, 0))],
        out_specs=[out_spec, out_spec],
        out_shape=[jax.ShapeDtypeStruct((batch, m * dilation, w), F32)] * 2,
        scratch_shapes=[pltpu.VMEM((2, tq * dilation, LANES), F32)],
        compiler_params=_cparams("arbitrary", "arbitrary", "arbitrary"),
        name=f"band_att_d{dilation}",
    )(q, k, k, k, v, v, v, bias)


def _hy_pre_kernel(x0_ref, x1_ref, v_ref, w_ref, b_ref, z_ref, x0c_ref):
    L, tn = x0_ref.shape[1], x0_ref.shape[2]
    row = lax.broadcasted_iota(jnp.int32, (L, tn), 0)

    def conv(ref, s):
        u = ref[0].astype(F32)
        um = jnp.where(row == 0, 0.0, pltpu.roll(u, 1, 0))
        up = jnp.where(row == L - 1, 0.0, pltpu.roll(u, L - 1, 0))
        return um * w_ref[0, s:s + 1, :] + u * w_ref[1, s:s + 1, :] + up * w_ref[2, s:s + 1, :] + b_ref[s:s + 1, :]

    x0c_ref[0] = conv(x0_ref, 0).astype(BF16)
    z_ref[0] = (conv(v_ref, 2) * conv(x1_ref, 1)).astype(BF16)


def _hy_pre(hy, short_w, short_b):
    b, L, n3 = hy.shape
    c = n3 // 3
    tn = LANES
    nt = c // tn
    w3 = short_w.reshape(3, 3, c)
    b3 = short_b.reshape(3, c)
    stream = lambda s: pl.BlockSpec((1, L, tn), lambda bi, n: (bi, 0, s * nt + n))
    return pl.pallas_call(
        _hy_pre_kernel,
        grid=(b, nt),
        in_specs=[stream(0), stream(1), stream(2),
                  pl.BlockSpec((3, 3, tn), lambda bi, n: (0, 0, n)),
                  pl.BlockSpec((3, tn), lambda bi, n: (0, n))],
        out_specs=[pl.BlockSpec((1, L, tn), lambda bi, n: (bi, 0, n))] * 2,
        out_shape=[jax.ShapeDtypeStruct((b, L, c), BF16)] * 2,
        compiler_params=_cparams("arbitrary", "arbitrary"),
        name="hy_pre",
    )(hy, hy, hy, w3, b3)


def _filt_kernel(z_ref, w1_ref, b1_ref, w2_ref, b2_ref, w3_ref, b3_ref, sf_ref, dec_ref, o_ref):
    z = z_ref[...]
    tl = z.shape[0]
    c = dec_ref.shape[1]
    h = jnp.sin(sf_ref[0:1, :] * (_dot(z, w1_ref[...], precision=HIGHEST) + b1_ref[...]))
    h = jnp.sin(sf_ref[1:2, :] * (_dot(h, w2_ref[...], precision=HIGHEST) + b2_ref[...]))
    hf = _dot(h, w3_ref[...], precision=HIGHEST) + b3_ref[...]
    t = z[:, 0:1]
    dec = jnp.abs(dec_ref[...])
    fwd = hf[:, :c] * jnp.exp(-t * dec[0:1, :])
    bwd = hf[:, c:] * jnp.exp(-t * dec[1:2, :])
    row = pl.program_id(0) * tl + lax.broadcasted_iota(jnp.int32, (tl, c), 0)
    o_ref[0] = fwd.astype(BF16)
    o_ref[1] = jnp.where(row == 0, 0.0, bwd).astype(BF16)


def _position_embedding(L):
    t = jnp.linspace(0.0, 1.0, L, dtype=F32)[:, None]
    bands = (HYENA_EMB_DIM - 1) // 2
    freqs = jnp.linspace(1e-4, bands - 1, bands, dtype=F32)[None, :]
    wpos = 2.0 * math.pi * jnp.arange(L, dtype=F32)[:, None] / L
    z = jnp.concatenate([t, jnp.cos(freqs * wpos), -jnp.sin(freqs * wpos)], axis=-1)
    return jnp.pad(z, ((0, 0), (0, LANES - HYENA_EMB_DIM)))


def _hyena_filter_taps(L, w1, b1, w2, b2, w3, b3, sin_freq, decay):
    fw = w1.shape[1]
    c = decay.shape[1]
    tl = 512
    zemb = _position_embedding(L)
    w1p = jnp.pad(w1, ((0, LANES - HYENA_EMB_DIM), (0, 0)))
    full = lambda shape: pl.BlockSpec(shape, lambda i: (0,) * len(shape))
    return pl.pallas_call(
        _filt_kernel,
        grid=(L // tl,),
        in_specs=[pl.BlockSpec((tl, LANES), lambda i: (i, 0)),
                  full((LANES, fw)), full((1, fw)), full((fw, fw)), full((1, fw)),
                  full((fw, 2 * c)), full((1, 2 * c)), full((2, fw)), full((2, c))],
        out_specs=pl.BlockSpec((2, tl, c), lambda i: (0, i, 0)),
        out_shape=jax.ShapeDtypeStruct((2, L, c), BF16),
        compiler_params=_cparams("arbitrary"),
        name="hy_filter_taps",
    )(zemb, w1p, b1.reshape(1, fw), w2, b2.reshape(1, fw), w3, b3.reshape(1, 2 * c), sin_freq, decay)


def _dft_matrix_kernel(coarse_ref, fine_ref, o_ref):
    ca, sa = coarse_ref[0, 0], coarse_ref[0, 1]
    cb, sb = fine_ref[0], fine_ref[1]
    o_ref[0] = (ca * cb - sa * sb).astype(BF16)
    o_ref[1] = (sa * cb + ca * sb).astype(BF16)


def _shifted_dft_matrix(L):
    nk1 = L // LANES
    odd = 2 * jnp.arange(L, dtype=jnp.int32) + 1
    unit = 2.0 * math.pi / (8 * L)
    ang_a = ((2 * LANES * jnp.arange(nk1, dtype=jnp.int32)[:, None] * odd[None, :]) % (8 * L)).astype(F32) * unit
    ang_b = ((odd[:LANES, None] * odd[None, :]) % (8 * L)).astype(F32) * unit
    coarse = jnp.stack([jnp.cos(ang_a), jnp.sin(ang_a)], axis=1).reshape(nk1, 2, 1, L)
    fine = jnp.stack([jnp.cos(ang_b), jnp.sin(ang_b)])
    return pl.pallas_call(
        _dft_matrix_kernel,
        grid=(nk1,),
        in_specs=[pl.BlockSpec((1, 2, 1, L), lambda i: (i, 0, 0, 0)),
                  pl.BlockSpec((2, LANES, L), lambda i: (0, 0, 0))],
        out_specs=pl.BlockSpec((2, LANES, L), lambda i: (0, i, 0)),
        out_shape=jax.ShapeDtypeStruct((2, L, L), BF16),
        compiler_params=_cparams("arbitrary"),
        name="dft_matrix",
    )(coarse, fine)


def _filt_dft_kernel(m_ref, r_ref, ct_ref, st_ref, h_ref):
    a = _dot(m_ref[0], r_ref[0])
    b = _dot(m_ref[1], r_ref[0])
    c = _dot(m_ref[0], r_ref[1])
    d = _dot(m_ref[1], r_ref[1])
    ct, st = ct_ref[...], st_ref[...]
    h_ref[0] = ct * (a + c) + st * (b + d)
    h_ref[1] = st * (a - c) + ct * (d - b)


def _filter_spectrum(dft, taps):
    _, L, c = taps.shape
    tk, tn = 256, min(256, c)
    theta = (2 * jnp.arange(L, dtype=F32) + 1) * (math.pi / (4 * L))
    ct = jnp.cos(theta)[:, None]
    st = jnp.sin(theta)[:, None]
    return pl.pallas_call(
        _filt_dft_kernel,
        grid=(c // tn, L // tk),
        in_specs=[pl.BlockSpec((2, tk, L), lambda n, k: (0, k, 0)),
                  pl.BlockSpec((2, L, tn), lambda n, k: (0, 0, n)),
                  pl.BlockSpec((tk, 1), lambda n, k: (k, 0)),
                  pl.BlockSpec((tk, 1), lambda n, k: (k, 0))],
        out_specs=pl.BlockSpec((2, tk, tn), lambda n, k: (0, k, n)),
        out_shape=jax.ShapeDtypeStruct((2, L, c), F32),
        compiler_params=_cparams("arbitrary", "arbitrary"),
        name="hy_filter_dft",
    )(dft, taps, ct, st)


def _hy_fwd_kernel(m_ref, z_ref, h_ref, y_ref):
    zc = _dot(m_ref[0], z_ref[0])
    zs = _dot(m_ref[1], z_ref[0])
    hr, hi = h_ref[0], h_ref[1]
    y_ref[0, 0] = (zc * hr + zs * hi).astype(BF16)
    y_ref[0, 1] = (zc * hi - zs * hr).astype(BF16)


def _hy_forward(dft, z, spec):
    b, L, c = z.shape
    tk, tn = 256, min(512, c)
    return pl.pallas_call(
        _hy_fwd_kernel,
        grid=(b, c // tn, L // tk),
        in_specs=[pl.BlockSpec((2, tk, L), lambda bi, n, k: (0, k, 0)),
                  pl.BlockSpec((1, L, tn), lambda bi, n, k: (bi, 0, n)),
                  pl.BlockSpec((2, tk, tn), lambda bi, n, k: (0, k, n))],
        out_specs=pl.BlockSpec((1, 2, tk, tn), lambda bi, n, k: (bi, 0, k, n)),
        out_shape=jax.ShapeDtypeStruct((b, 2, L, c), BF16),
        compiler_params=_cparams("arbitrary", "arbitrary", "arbitrary"),
        name="hy_dft_fwd",
    )(dft, z, spec)


def _hy_inv_kernel(m_ref, y_ref, z_ref, x0_ref, bd_ref, o_ref, *, scale):
    conv = _dot(m_ref[0], y_ref[0, 0]) - _dot(m_ref[1], y_ref[0, 1])
    y = conv * scale + z_ref[0].astype(F32) * bd_ref[...]
    o_ref[0] = x0_ref[0].astype(F32) * y


def _hy_inverse(dft, yspec, z, x0c, bias_d):
    b, L, c = z.shape
    tt, tn = 256, min(512, c)
    return pl.pallas_call(
        functools.partial(_hy_inv_kernel, scale=1.0 / L),
        grid=(b, c // tn, L // tt),
        in_specs=[pl.BlockSpec((2, tt, L), lambda bi, n, t: (0, t, 0)),
                  pl.BlockSpec((1, 2, L, tn), lambda bi, n, t: (bi, 0, 0, n)),
                  pl.BlockSpec((1, tt, tn), lambda bi, n, t: (bi, t, n)),
                  pl.BlockSpec((1, tt, tn), lambda bi, n, t: (bi, t, n)),
                  pl.BlockSpec((1, tn), lambda bi, n, t: (0, n))],
        out_specs=pl.BlockSpec((1, tt, tn), lambda bi, n, t: (bi, t, n)),
        out_shape=jax.ShapeDtypeStruct((b, L, c), F32),
        compiler_params=_cparams("arbitrary", "arbitrary", "arbitrary"),
        name="hy_dft_inv",
    )(dft, yspec, z, x0c, bias_d.reshape(1, c))


def _out_kernel(o1_ref, o2_ref, o3_ref, l1_ref, l2_ref, l3_ref, hy_ref, x_ref, ada_ref,
                ga_ref, gh_ref, w_ref, g2_ref, x1_ref, h2_ref):
    la, lb, lc = l1_ref[0], l2_ref[0], l3_ref[0]
    lm = jnp.maximum(jnp.maximum(la, lb), lc)
    wa, wb, wc = jnp.exp(la - lm), jnp.exp(lb - lm), jnp.exp(lc - lm)
    att = (wa * o1_ref[0] + wb * o2_ref[0] + wc * o3_ref[0]) / (wa + wb + wc)
    wa_ = att.shape[1]
    an = (_rms(att) * ga_ref[...]).astype(BF16)
    hn = (_rms(hy_ref[0]) * gh_ref[...]).astype(BF16)
    res = _dot(an, w_ref[:wa_, :]) + _dot(hn, w_ref[wa_:, :])
    x1 = x_ref[0] + ada_ref[0, 2:3, :] * res
    x1_ref[0] = x1
    h2_ref[0] = _modulated_norm(x1, g2_ref[...], ada_ref[0, 4:5, :], ada_ref[0, 3:4, :]).astype(BF16)


def _out_proj(att_parts, hyo, x, ada_g, att_g, hy_g, w_out, norm2_g):
    b, L, d = x.shape
    wa = hyo.shape[2]
    tm = 256
    tok = lambda width: pl.BlockSpec((1, tm, width), lambda bi, i: (bi, i, 0))
    full = lambda shape: pl.BlockSpec(shape, lambda bi, i: (0,) * len(shape))
    (o1, l1), (o2, l2), (o3, l3) = att_parts
    return pl.pallas_call(
        _out_kernel,
        grid=(b, L // tm),
        in_specs=[tok(wa)] * 7 + [tok(d), pl.BlockSpec((1, 6, d), lambda bi, i: (bi, 0, 0)),
                                  full((1, wa)), full((1, wa)), full((d, d)), full((1, d))],
        out_specs=[tok(d), tok(d)],
        out_shape=[jax.ShapeDtypeStruct((b, L, d), F32), jax.ShapeDtypeStruct((b, L, d), BF16)],
        compiler_params=_cparams("arbitrary", "arbitrary"),
        name="out_proj",
    )(o1, o2, o3, l1, l2, l3, hyo, x, ada_g, att_g, hy_g, w_out, norm2_g)


def _peer_q_kernel(h_ref, w_ref, sk_ref, s_ref):
    q = _dot(h_ref[...], w_ref[...]).astype(BF16)
    for c in range(sk_ref.shape[0]):
        qc = q[:, c * PEER_KEYS:(c + 1) * PEER_KEYS]
        s_ref[c] = lax.dot_general(sk_ref[c], qc, (((1,), (1,)), ((), ())), preferred_element_type=F32)


def _peer_scores(h2, w_query, sub_keys):
    t, d = h2.shape
    nc, nk, qd = sub_keys.shape
    tm = 512
    return pl.pallas_call(
        _peer_q_kernel,
        grid=(t // tm,),
        in_specs=[pl.BlockSpec((tm, d), lambda i: (i, 0)),
                  pl.BlockSpec((d, nc * qd), lambda i: (0, 0)),
                  pl.BlockSpec((nc, nk, qd), lambda i: (0, 0, 0))],
        out_specs=pl.BlockSpec((nc, nk, tm), lambda i: (0, 0, i)),
        out_shape=jax.ShapeDtypeStruct((nc, nk, t), F32),
        compiler_params=_cparams("arbitrary"),
        name="peer_scores",
    )(h2, w_query, sub_keys)


def _gelu_tanh(x):
    return x * (0.5 * (1.0 + jnp.tanh(math.sqrt(2.0 / math.pi) * (x + 0.044715 * (x * x * x)))))


def _peer_main_kernel(ht_ref, u_ref, vt_ref, th_ref, s2_ref, e1_ref, e2_ref, o_ref, at_ref, pt_ref):
    e = pl.program_id(1)
    te = u_ref.shape[0]
    ni = te // PEER_KEYS

    @pl.when(e == 0)
    def _():
        o_ref[...] = jnp.zeros_like(o_ref)

    at_ref[...] = _dot(u_ref[...], ht_ref[...])
    for ii in range(ni):
        idx = e * ni + ii
        rows = slice(ii * PEER_KEYS, (ii + 1) * PEER_KEYS)
        for c in range(ht_ref.shape[1] // LANES):
            cols = slice(c * LANES, (c + 1) * LANES)
            w = jnp.zeros((PEER_KEYS, LANES), F32)
            for h in range(PEER_HEADS):
                chosen = s2_ref[h, c] >= th_ref[h, c, pl.ds(idx, 1), :]
                gate = e1_ref[h, c, pl.ds(idx, 1), :] * e2_ref[h, c]
                w = w + jnp.where(chosen, gate, 0.0)
            pt_ref[rows, cols] = (_gelu_tanh(at_ref[rows, cols]) * w).astype(BF16)
    o_ref[...] += _dot(vt_ref[0], pt_ref[...])


def _peer_experts(h2t, u_tab, vt_tab, theta, s2m, e1, e2):
    d, t = h2t.shape
    ne = u_tab.shape[0]
    tm, te = 512, PEER_EXPERT_TILE
    nt = ne // te
    gate =pl.BlockSpec((PEER_HEADS, tm // LANES, PEER_KEYS, LANES), lambda i, e: (0, i, 0, 0))
    return pl.pallas_call(
        _peer_main_kernel,
        grid=(t // tm, nt),
        in_specs=[pl.BlockSpec((d, tm), lambda i, e: (0, i)),
                  pl.BlockSpec((te, d), lambda i, e: (e, 0)),
                  pl.BlockSpec((1, d, te), lambda i, e: (e, 0, 0)),
                  gate, gate, gate, gate],
        out_specs=pl.BlockSpec((d, tm), lambda i, e: (0, i)),
        out_shape=jax.ShapeDtypeStruct((d, t), F32),
        scratch_shapes=[pltpu.VMEM((te, tm), F32), pltpu.VMEM((te, tm), BF16)],
        compiler_params=_cparams("arbitrary", "arbitrary"),
        name="peer_experts",
    )(h2t, u_tab, vt_tab, theta, s2m, e1, e2)


def _fin_kernel(ot_ref, x1_ref, ada_ref, y_ref):
    y_ref[...] = x1_ref[...] + ada_ref[0, 5:6, :] * ot_ref[...].T


def _final_residual(out_t, x1, ada_g, seq_len):
    t, d = x1.shape
    tm = 256
    per_seq = seq_len // tm
    return pl.pallas_call(
        _fin_kernel,
        grid=(t // tm,),
        in_specs=[pl.BlockSpec((d, tm), lambda i: (0, i)),
                  pl.BlockSpec((tm, d), lambda i: (i, 0)),
                  pl.BlockSpec((1, 6, d), lambda i: (i // per_seq, 0, 0))],
        out_specs=pl.BlockSpec((tm, d), lambda i: (i, 0)),
        out_shape=jax.ShapeDtypeStruct((t, d), F32),
        compiler_params=_cparams("arbitrary"),
        name="final_residual",
    )(out_t, x1, ada_g)


def _extract_top(work_ref, vals_ref, count, idx_ref=None):
    n, tm = work_ref.shape
    rows = lax.broadcasted_iota(jnp.int32, (n, tm), 0)

    def body(r, carry):
        s = work_ref[...]
        m = jnp.max(s, axis=0, keepdims=True)
        idx = jnp.min(jnp.where(s == m, rows, n), axis=0, keepdims=True)
        work_ref[...] = jnp.where(rows == idx, MASKED, s)
        vals_ref[pl.ds(r, 1), :] = m
        if idx_ref is not None:
            idx_ref[pl.ds(r, 1), :] = idx
        return carry

    lax.fori_loop(0, count, body, 0)


_CAND_SECOND = 8
_CAND_ROWS = PEER_TOPK + 7 * _CAND_SECOND + (PEER_TOPK - 8)


def _peer_gate_kernel(s_ref, th_ref, s2m_ref, e1_ref, e2_ref, work_ref, v1_ref, v2_ref,
                      cand_ref, best_ref, idx_ref):
    k = PEER_TOPK
    s1 = s_ref[0]
    s2 = s_ref[1]
    work_ref[...] = s1
    _extract_top(work_ref, v1_ref, k, idx_ref)
    in1 = work_ref[...] == MASKED
    work_ref[...] = s2
    _extract_top(work_ref, v2_ref, k)
    in2 = work_ref[...] == MASKED

    v1 = v1_ref[...]
    v2 = v2_ref[...]
    cand_ref[0:k, :] = v1[0:1] + v2
    brow = lax.broadcasted_iota(jnp.int32, (_CAND_SECOND, v2.shape[1]), 0)
    for a in range(1, 8):
        blk = v1[a:a + 1] + v2[0:_CAND_SECOND]
        lo = k + (a - 1) * _CAND_SECOND
        cand_ref[lo:lo + _CAND_SECOND, :] = jnp.where(brow < k // (a + 1), blk, MASKED)
    cand_ref[k + 7 * _CAND_SECOND:, :] = v1[8:k] + v2[0:1]
    _extract_top(cand_ref, best_ref, k)

    best = best_ref[...]
    zsum = jnp.sum(jnp.exp(best - best[0:1]), axis=0, keepdims=True)
    tau = best[k - 1:k]

    rows = lax.broadcasted_iota(jnp.int32, s1.shape, 0)
    theta = jnp.full(s1.shape, UNSELECTED, F32)
    for a in range(k):
        th_a = jnp.min(jnp.where(v1[a:a + 1] + v2 >= tau, v2, UNSELECTED), axis=0, keepdims=True)
        theta = jnp.where(rows == idx_ref[a:a + 1, :], th_a, theta)

    s2m = jnp.where(in2, s2, MASKED)
    e1 = jnp.where(in1, jnp.exp(s1 - v1[0:1]) / zsum, 0.0)
    e2 = jnp.where(in2, jnp.exp(s2 - v2[0:1]), 0.0)
    for c in range(s1.shape[1] // LANES):
        cols = slice(c * LANES, (c + 1) * LANES)
        th_ref[0, c] = theta[:, cols]
        s2m_ref[0, c] = s2m[:, cols]
        e1_ref[0, c] = e1[:, cols]
        e2_ref[0, c] = e2[:, cols]


def _peer_gates(scores):
    nc, nk, t = scores.shape
    nh = nc // 2
    tm = 512
    ncl = tm // LANES
    big = pl.BlockSpec((1, ncl, nk, LANES), lambda i, h: (h, i, 0, 0))
    big_shape = jax.ShapeDtypeStruct((nh, t // LANES, nk, LANES), F32)
    return pl.pallas_call(
        _peer_gate_kernel,
        grid=(t // tm, nh),
        in_specs=[pl.BlockSpec((2, nk, tm), lambda i, h: (h, 0, i))],
        out_specs=[big, big, big, big],
        out_shape=[big_shape] * 4,
        scratch_shapes=[pltpu.VMEM((nk, tm), F32), pltpu.VMEM((PEER_TOPK, tm), F32),
                        pltpu.VMEM((PEER_TOPK, tm), F32), pltpu.VMEM((_CAND_ROWS, tm), F32),
                        pltpu.VMEM((PEER_TOPK, tm), F32), pltpu.VMEM((PEER_TOPK, tm), jnp.int32)],
        compiler_params=_cparams("arbitrary", "arbitrary"),
        name="peer_gates",
    )(scores)


def _encode_group(x, ada_g, prm):
    b, L, d = x.shape
    (q1, q4, q16), (k1, k4, k16), (v1, v4, v16) = _qkv_proj(
        x, ada_g, prm["norm1_g"], prm["w_qkv"], prm["gq"], prm["gk"], prm["ones_bd"])
    att_parts = [
        _band_attention(q1[:, None], k1[:, None], v1[:, None], prm["bias"][0]),
        _band_attention(q4, k4, v4, prm["bias"][1]),
        _band_attention(q16, k16, v16, prm["bias"][2]),
    ]
    hy = _hy_proj(x, ada_g, prm["norm1_g"], prm["w_hy"])
    z, x0c = _hy_pre(hy, prm["short_w"], prm["short_b"])
    dft = _shifted_dft_matrix(L)
    taps = _hyena_filter_taps(L, *prm["filter"])
    spec = _filter_spectrum(dft, taps)
    yspec = _hy_forward(dft, z, spec)
    hyo = _hy_inverse(dft, yspec, z, x0c, prm["bias_d"])
    x1, h2 = _out_proj(att_parts, hyo, x, ada_g, prm["att_out_g"], prm["hyena_out_g"], prm["w_out"], prm["norm2_g"])
    t = b * L
    h2 = h2.reshape(t, d)
    scores = _peer_scores(h2, prm["w_query"], prm["sub_keys"])
    theta, s2m, e1, e2 = _peer_gates(scores)
    out_t = _peer_experts(h2.T, prm["u_tab"], prm["vt_tab"], theta, s2m, e1, e2)
    y = _final_residual(out_t, x1.reshape(t, d), ada_g, L)
    return y.reshape(b, L, d)


def kernel(x_prompt, x_sample, c_prompt, c_sample, rel_bias, w_ada, b_ada, norm1_g, w_in, q_norm_g, k_norm_g, hyena_short_w, hyena_short_b, hyena_ffn_w1, hyena_ffn_b1, hyena_ffn_w2, hyena_ffn_b2, hyena_ffn_w3, hyena_ffn_b3, hyena_sin_freq, hyena_decay, hyena_bias_d, att_out_g, hyena_out_g, w_out, norm2_g, peer_w_query, peer_sub_keys, peer_u, peer_v):
    assert w_ada.shape[0] == 1, "single-layer problem"
    d = x_prompt.shape[-1]
    bp, bs = x_prompt.shape[0], x_sample.shape[0]
    att_w = att_out_g.shape[1]
    nheads = att_w // HEAD_DIM

    c_all = jnp.concatenate([c_prompt, c_sample], axis=0)
    pad = (-c_all.shape[0]) % 8
    ada = _ada(jnp.pad(c_all, ((0, pad), (0, 0))), w_ada[0], b_ada[0])
    ada = ada[:bp + bs].reshape(bp + bs, 6, d)

    w_in0 = w_in[0]
    lane = jnp.arange(LANES)
    prm = {
        "norm1_g": norm1_g[0].reshape(1, d),
        "w_qkv": w_in0[:, :3 * att_w].astype(BF16),
        "w_hy": w_in0[:, 3 * att_w:].astype(BF16),
        "gq": jnp.tile(q_norm_g[0], nheads).reshape(1, att_w),
        "gk": jnp.tile(k_norm_g[0], nheads).reshape(1, att_w),
        "ones_bd": (lane[:, None] // HEAD_DIM == lane[None, :] // HEAD_DIM).astype(BF16),
        "bias": [_band_bias(rel_bias, dil) for dil in DILATIONS],
        "short_w": hyena_short_w[0],
        "short_b": hyena_short_b[0],
        "filter": (hyena_ffn_w1[0], hyena_ffn_b1[0], hyena_ffn_w2[0], hyena_ffn_b2[0],
                   hyena_ffn_w3[0], hyena_ffn_b3[0], hyena_sin_freq[0], hyena_decay[0]),
        "bias_d": hyena_bias_d[0],
        "att_out_g": att_out_g[0].reshape(1, -1),
        "hyena_out_g": hyena_out_g[0].reshape(1, -1),
        "w_out": w_out[0].astype(BF16),
        "norm2_g": norm2_g[0].reshape(1, d),
        "w_query": peer_w_query[0].astype(BF16),
        "sub_keys": peer_sub_keys[0].reshape(2 * PEER_HEADS, PEER_KEYS, -1).astype(BF16),
        "u_tab": peer_u[0].astype(BF16),
        "vt_tab": peer_v[0].astype(BF16).reshape(-1, PEER_EXPERT_TILE, d).transpose(0, 2, 1),
    }
    y_prompt = _encode_group(x_prompt, ada[:bp], prm)
    y_sample = _encode_group(x_sample, ada[bp:], prm)
    return (y_prompt, y_sample)
```
